```python
import math
import jax, jax.numpy as jnp
from jax import lax
import numpy as np

D_MODEL = 1024
BATCH = 8
SEQ = 8192
DEPTH = 2

N_META = 16
CHUNK = 64
PAD = CHUNK - N_META
N_MIXERS = 2
RMS_EPS = 1e-6
RET_HEADS = 4
RET_DK = 256
RET_DV = 512
RET_QK = RET_HEADS * RET_DK
RET_V = RET_HEADS * RET_DV
RET_IN = 2 * RET_QK + 2 * RET_V
ROPE_BASE = 10000.0
DN_HEADS = 8
DN_DK = 128
DN_DV = 256
DN_QK = DN_HEADS * DN_DK
DN_V = DN_HEADS * DN_DV
DN_CONV_CH = 2 * DN_QK + DN_V
DN_IN = DN_CONV_CH + DN_V + 2 * DN_HEADS
CONV_K = 4
FFN_HIDDEN = ((8 * D_MODEL + 3 * 256 - 1) // (3 * 256)) * 256
N_RET_LAYERS = (DEPTH + 1) // 2
N_DN_LAYERS = DEPTH // 2

kernel_name = "hybrid_retention_gated_deltanet_meta"


def rmsnorm(x, w):
    xf = x.astype(jnp.float32)
    y = xf * lax.rsqrt(jnp.mean(xf * xf, axis=-1, keepdims=True) + RMS_EPS)
    return (y * w.astype(jnp.float32)).astype(x.dtype)


def l2norm(x):
    return x * lax.rsqrt(jnp.sum(x * x, axis=-1, keepdims=True) + RMS_EPS)


def rope(t, pos):
    half = t.shape[-1] // 2
    inv_freq = ROPE_BASE ** (-jnp.arange(half, dtype=jnp.float32) / half)
    ang = pos[:, None] * inv_freq[None, :]
    cos = jnp.cos(ang)[None, :, None, :]
    sin = jnp.sin(ang)[None, :, None, :]
    t1, t2 = t[..., :half], t[..., half:]
    return jnp.concatenate([t1 * cos - t2 * sin, t1 * sin + t2 * cos], axis=-1)


def to_chunks(t):
    b, l, h, d = t.shape
    return t.reshape(b, l // CHUNK, CHUNK, h, d).transpose(1, 0, 3, 2, 4)


def from_chunks(t):
    n, b, h, c, d = t.shape
    return t.transpose(1, 0, 3, 2, 4).reshape(b, n * c, h, d)


def causal_conv(x, w):
    ch = x.shape[-1]
    return lax.conv_general_dilated(
        x, w.astype(x.dtype)[:, None, :], window_strides=(1,), padding=[(CONV_K - 1, 0)],
        dimension_numbers=("NWC", "WIO", "NWC"), feature_group_count=ch)


def gated_head_norm(o, norm_w, gate):
    o = o * lax.rsqrt(jnp.mean(o * o, axis=-1, keepdims=True) + RMS_EPS) * norm_w.astype(jnp.float32)
    return o * jax.nn.silu(gate.astype(jnp.float32))


def retention(h, w_in, gn_w, w_out, valid, pos):
    b, l, _ = h.shape
    proj = h @ w_in
    q, k, v, g = jnp.split(proj, [RET_QK, 2 * RET_QK, 2 * RET_QK + RET_V], axis=-1)
    q = rope(q.reshape(b, l, RET_HEADS, RET_DK).astype(jnp.float32), pos)
    k = rope(k.reshape(b, l, RET_HEADS, RET_DK).astype(jnp.float32), pos)
    k = k * (RET_DK ** -0.5) * valid[None, :, None, None]
    v = v.reshape(b, l, RET_HEADS, RET_DV).astype(jnp.float32)

    log_gamma = jnp.log1p(-jnp.exp2(-5.0 - jnp.arange(RET_HEADS, dtype=jnp.float32)))
    idx = jnp.arange(CHUNK, dtype=jnp.float32)
    rel = idx[:, None] - idx[None, :]
    dmask = jnp.where((rel >= 0)[None], jnp.exp(log_gamma[:, None, None] * jnp.maximum(rel, 0.0)), 0.0)
    xi = jnp.exp(log_gamma[:, None] * (idx[None, :] + 1.0))[:, :, None]
    zeta = jnp.exp(log_gamma[:, None] * (CHUNK - 1.0 - idx[None, :]))[:, :, None]
    gamma_c = jnp.exp(log_gamma * CHUNK)[:, None, None]

    def step(state, inp):
        qc, kc, vc = inp
        scores = jnp.einsum("bhid,bhjd->bhij", qc, kc) * dmask
        o = jnp.einsum("bhij,bhjv->bhiv", scores, vc) + jnp.einsum("bhid,bhdv->bhiv", qc * xi, state)
        state = gamma_c * state + jnp.einsum("bhjd,bhjv->bhdv", kc * zeta, vc)
        return state, o

    s0 = jnp.zeros((b, RET_HEADS, RET_DK, RET_DV), jnp.float32)
    _, o = lax.scan(step, s0, (to_chunks(q), to_chunks(k), to_chunks(v)))
    o = from_chunks(o)
    o = gated_head_norm(o, gn_w, g.reshape(b, l, RET_HEADS, RET_DV))
    return o.reshape(b, l, RET_V).astype(h.dtype) @ w_out


def gated_deltanet(h, w_in, conv_w, a_log, dt_bias, norm_w, w_out, valid):
    b, l, _ = h.shape
    proj = h @ w_in
    qkv, gate, beta_in, a_in = jnp.split(
        proj, [DN_CONV_CH, DN_CONV_CH + DN_V, DN_CONV_CH + DN_V + DN_HEADS], axis=-1)
    qkv = qkv * valid[None, :, None].astype(qkv.dtype)
    qkv = jax.nn.silu(causal_conv(qkv, conv_w))
    q, k, v = jnp.split(qkv, [DN_QK, 2 * DN_QK], axis=-1)
    q = l2norm(q.reshape(b, l, DN_HEADS, DN_DK).astype(jnp.float32)) * (DN_DK ** -0.5)
    k = l2norm(k.reshape(b, l, DN_HEADS, DN_DK).astype(jnp.float32))
    v = v.reshape(b, l, DN_HEADS, DN_DV).astype(jnp.float32)
    vmask = valid[None, :, None]
    beta = (jax.nn.sigmoid(beta_in.astype(jnp.float32)) * vmask)[..., None]
    g = (-jnp.exp(a_log.astype(jnp.float32))
         * jax.nn.softplus(a_in.astype(jnp.float32) + dt_bias.astype(jnp.float32)) * vmask)[..., None]

    incl = jnp.tril(jnp.ones((CHUNK, CHUNK), dtype=bool))
    strict = jnp.tril(jnp.ones((CHUNK, CHUNK), dtype=bool), -1)
    eye = jnp.eye(CHUNK, dtype=jnp.float32)

    def step(state, inp):
        qc, kc, vc, bc, gc = inp
        gam = jnp.cumsum(gc, axis=-2)
        diff = gam - jnp.swapaxes(gam, -1, -2)
        decay = jnp.exp(jnp.where(incl, diff, -jnp.inf))
        kk = jnp.einsum("bhid,bhjd->bhij", kc, kc)
        a_mat = jnp.where(strict, bc * kk * decay, 0.0)
        rhs = jnp.concatenate([vc * bc, kc * bc * jnp.exp(gam)], axis=-1)
        sol = lax.linalg.triangular_solve(eye + a_mat, rhs, left_side=True, lower=True,
                                          unit_diagonal=True)
        u, w = sol[..., :DN_DV], sol[..., DN_DV:]
        v_new = u - jnp.einsum("bhik,bhkv->bhiv", w, state)
        qk = jnp.einsum("bhid,bhjd->bhij", qc, kc) * decay
        o = (jnp.einsum("bhid,bhdv->bhiv", qc * jnp.exp(gam), state)
             + jnp.einsum("bhij,bhjv->bhiv", qk, v_new))
        g_last = gam[..., -1:, :]
        state = state * jnp.exp(g_last) + jnp.einsum(
            "bhjd,bhjv->bhdv", kc * jnp.exp(g_last - gam), v_new)
        return state, o

    s0 = jnp.zeros((b, DN_HEADS, DN_DK, DN_DV), jnp.float32)
    _, o = lax.scan(step, s0, (to_chunks(q), to_chunks(k), to_chunks(v),
                               to_chunks(beta), to_chunks(g)))
    o = from_chunks(o)
    o = gated_head_norm(o, norm_w, gate.reshape(b, l, DN_HEADS, DN_DV))
    return o.reshape(b, l, DN_V).astype(h.dtype) @ w_out


def swiglu(h, w_gate, w_up, w_down):
    return (jax.nn.silu(h @ w_gate) * (h @ w_up)) @ w_down


def setup_inputs(seed: int = 0) -> dict:
    key = jax.random.key(seed)
    ks = jax.random.split(key, 20)
    f32 = jnp.float32
    nrm = lambda k, shape, fan_in: jax.random.normal(k, shape, f32) * (fan_in ** -0.5)
    gain = lambda k, shape: 1.0 + 0.02 * jax.random.normal(k, shape, f32)
    dt = jnp.exp(jax.random.uniform(ks[9], (N_DN_LAYERS, DN_HEADS), f32)
                 * (math.log(0.1) - math.log(0.001)) + math.log(0.001))
    return {
        "x": jax.random.normal(ks[0], (BATCH, SEQ, D_MODEL), f32),
        "meta_tokens": jax.random.normal(ks[1], (N_META, D_MODEL), f32),
        "mix_norm_w": gain(ks[2], (DEPTH, D_MODEL)),
        "ffn_norm_w": gain(ks[3], (DEPTH, D_MODEL)),
        "ret_w_in": nrm(ks[4], (N_RET_LAYERS, D_MODEL, RET_IN), D_MODEL),
        "ret_gn_w": gain(ks[5], (N_RET_LAYERS, RET_DV)),
        "ret_w_out": nrm(ks[6], (N_RET_LAYERS, RET_V, D_MODEL), RET_V),
        "dn_w_in": nrm(ks[7], (N_DN_LAYERS, D_MODEL, DN_IN), D_MODEL),
        "dn_conv_w": nrm(ks[8], (N_DN_LAYERS, CONV_K, DN_CONV_CH), CONV_K),
        "dn_a_log": jnp.log(jax.random.uniform(ks[10], (N_DN_LAYERS, DN_HEADS), f32, 1.0, 16.0)),
        "dn_dt_bias": dt + jnp.log(-jnp.expm1(-dt)),
        "dn_norm_w": gain(ks[11], (N_DN_LAYERS, DN_DV)),
        "dn_w_out": nrm(ks[12], (N_DN_LAYERS, DN_V, D_MODEL), DN_V),
        "ffn_w_gate": nrm(ks[13], (DEPTH, D_MODEL, FFN_HIDDEN), D_MODEL),
        "ffn_w_up": nrm(ks[14], (DEPTH, D_MODEL, FFN_HIDDEN), D_MODEL),
        "ffn_w_down": nrm(ks[15], (DEPTH, FFN_HIDDEN, D_MODEL), FFN_HIDDEN),
        "final_norm_w": gain(ks[16], (D_MODEL,)),
    }


def reference(x, meta_tokens, mix_norm_w, ffn_norm_w, ret_w_in, ret_gn_w, ret_w_out,
              dn_w_in, dn_conv_w, dn_a_log, dn_dt_bias, dn_norm_w, dn_w_out,
              ffn_w_gate, ffn_w_up, ffn_w_down, final_norm_w):
    b = x.shape[0]
    h = jnp.concatenate([
        jnp.zeros((b, PAD, D_MODEL), x.dtype),
        jnp.broadcast_to(meta_tokens.astype(x.dtype)[None], (b, N_META, D_MODEL)),
        x], axis=1)
    l = h.shape[1]
    pos_i = jnp.arange(l) - PAD
    valid = (pos_i >= 0).astype(jnp.float32)
    pos = pos_i.astype(jnp.float32)
    for i in range(DEPTH):
        hn = rmsnorm(h, mix_norm_w[i])
        if i % N_MIXERS == 0:
            j = i // N_MIXERS
            mix = retention(hn, ret_w_in[j], ret_gn_w[j], ret_w_out[j], valid, pos)
        else:
            j = i // N_MIXERS
            mix = gated_deltanet(hn, dn_w_in[j], dn_conv_w[j], dn_a_log[j], dn_dt_bias[j],
                                 dn_norm_w[j], dn_w_out[j], valid)
        h = h + mix
        h = h + swiglu(rmsnorm(h, ffn_norm_w[i]), ffn_w_gate[i], ffn_w_up[i], ffn_w_down[i])
    return rmsnorm(h, final_norm_w)[:, CHUNK:, :]
```

```python
import functools
import math

import jax
import jax.numpy as jnp
from jax import lax
from jax.experimental import pallas as pl
from jax.experimental.pallas import tpu as pltpu

F32 = jnp.float32
BF16 = jnp.bfloat16

N_META = 16
CHUNK = 64
PAD = CHUNK - N_META
RMS_EPS = 1e-6
RET_HEADS = 4
RET_DK = 256
RET_DV = 512
RET_QK = RET_HEADS * RET_DK
RET_V = RET_HEADS * RET_DV
ROPE_BASE = 10000.0
DN_HEADS = 8
DN_DK = 128
DN_DV = 256
DN_QK = DN_HEADS * DN_DK
DN_V = DN_HEADS * DN_DV
DN_CONV_CH = 2 * DN_QK + DN_V
CONV_K = 4

LANES = 128
SUBLANES = 8
VMEM_LIMIT_BYTES = 56 * 1024 * 1024
TOKEN_BLOCK = 512
PROJ_N_CHUNK = 1024
RET_CHUNK = 256
TRI_PRECISION = lax.Precision.HIGHEST


def _cparams(sem):
    return pltpu.CompilerParams(dimension_semantics=sem, vmem_limit_bytes=VMEM_LIMIT_BYTES)


def _resident(shape):
    nd = len(shape)
    return pl.BlockSpec(shape, lambda *_: (0,) * nd, pipeline_mode=pl.Buffered(1))


def _rms(x, w):
    ms = jnp.mean(x * x, axis=-1, keepdims=True)
    return x * lax.rsqrt(ms + RMS_EPS) * w


def _silu(x):
    return x * jax.nn.sigmoid(x)


def _dot(a, b, precision=None):
    return jnp.dot(a, b, preferred_element_type=F32, precision=precision)


def _dot_nt(a, b):
    return lax.dot_general(a, b, (((1,), (1,)), ((), ())), preferred_element_type=F32)


def _dot_tn(a, b):
    return lax.dot_general(a, b, (((0,), (0,)), ((), ())), preferred_element_type=F32)


def _rms_inproj_kernel(*refs, has_small):
    if has_small:
        h_ref, nw_ref, w_ref, ws_ref, o_ref, os_ref = refs
    else:
        h_ref, nw_ref, w_ref, o_ref = refs
    xn = _rms(h_ref[...], nw_ref[...]).astype(BF16)
    n_total = o_ref.shape[-1]
    for n0 in range(0, n_total, PROJ_N_CHUNK):
        n1 = min(n0 + PROJ_N_CHUNK, n_total)
        o_ref[:, n0:n1] = _dot(xn, w_ref[:, n0:n1]).astype(o_ref.dtype)
    if has_small:
        os_ref[...] = _dot(xn, ws_ref[...])


def rms_inproj(h2d, norm_w, w, w_small=None):
    m, d = h2d.shape
    n = w.shape[1]
    bm = min(TOKEN_BLOCK, m)
    has_small = w_small is not None
    in_specs = [pl.BlockSpec((bm, d), lambda i: (i, 0)), _resident((1, d)), _resident((d, n))]
    args = [h2d, norm_w.reshape(1, d), w]
    out_shape = [jax.ShapeDtypeStruct((m, n), BF16)]
    out_specs = [pl.BlockSpec((bm, n), lambda i: (i, 0))]
    if has_small:
        ns = w_small.shape[1]
        in_specs.append(_resident((d, ns)))
        args.append(w_small)
        out_shape.append(jax.ShapeDtypeStruct((m, ns), F32))
        out_specs.append(pl.BlockSpec((bm, ns), lambda i: (i, 0)))
    outs = pl.pallas_call(
        functools.partial(_rms_inproj_kernel, has_small=has_small),
        grid=(m // bm,), in_specs=in_specs, out_specs=out_specs, out_shape=out_shape,
        compiler_params=_cparams(("parallel",)), name="rms_inproj")(*args)
    return outs if has_small else outs[0]


def _out_ffn_kernel(*refs, final):
    if final:
        h_ref, o_ref, wo_ref, nw_ref, wg_ref, wu_ref, wd_ref, fw_ref, out_ref = refs
    else:
        h_ref, o_ref, wo_ref, nw_ref, wg_ref, wu_ref, wd_ref, out_ref = refs
    h1 = h_ref[...] + _dot(o_ref[...], wo_ref[...])
    xn = _rms(h1, nw_ref[...]).astype(BF16)
    act = (_silu(_dot(xn, wg_ref[...])) * _dot(xn, wu_ref[...])).astype(BF16)
    h2 = h1 + _dot(act, wd_ref[...])
    if final:
        h2 = _rms(h2, fw_ref[...])
    out_ref[...] = h2


def out_ffn(h2d, o2d, w_out, norm_w, w_gate, w_up, w_down, final_w=None):
    m, d = h2d.shape
    dv = o2d.shape[1]
    f = w_gate.shape[1]
    bm = min(TOKEN_BLOCK, m)
    final = final_w is not None
    row = lambda width: pl.BlockSpec((bm, width), lambda i: (i, 0))
    in_specs = [row(d), row(dv), _resident((dv, d)), _resident((1, d)),
                _resident((d, f)), _resident((d, f)), _resident((f, d))]
    args = [h2d, o2d, w_out, norm_w.reshape(1, d), w_gate, w_up, w_down]
    if final:
        in_specs.append(_resident((1, d)))
        args.append(final_w.reshape(1, d))
    return pl.pallas_call(
        functools.partial(_out_ffn_kernel, final=final),
        grid=(m // bm,), in_specs=in_specs, out_specs=row(d),
        out_shape=jax.ShapeDtypeStruct((m, d), F32),
        compiler_params=_cparams(("parallel",)), name="out_ffn")(*args)


def _retention_kernel(q_ref, k_ref, v_ref, g_ref, cos_ref, sin_ref, valid_ref, dmask_ref,
                      xi_ref, zeta_ref, gc_ref, gnw_ref, s0_ref, o_ref, sfin_ref, state):
    c = pl.program_id(2)

    @pl.when(c == 0)
    def _():
        state[...] = s0_ref[...]

    cos = cos_ref[...]
    sin = sin_ref[...]
    half = RET_DK // 2

    def rope(t_ref):
        t = t_ref[...].astype(F32)
        t1, t2 = t[:, :half], t[:, half:]
        return t1 * cos - t2 * sin, t1 * sin + t2 * cos

    q1, q2 = rope(q_ref)
    k1, k2 = rope(k_ref)
    valid = valid_ref[...]
    k1 = k1 * (RET_DK ** -0.5) * valid
    k2 = k2 * (RET_DK ** -0.5) * valid
    xi = xi_ref[...]
    zeta = zeta_ref[...]
    qb = jnp.concatenate([q1, q2], axis=-1).astype(BF16)
    kb = jnp.concatenate([k1, k2], axis=-1).astype(BF16)
    qx = jnp.concatenate([q1 * xi, q2 * xi], axis=-1).astype(BF16)
    kz = jnp.concatenate([k1 * zeta, k2 * zeta], axis=-1).astype(BF16)
    vb = v_ref[...]

    s_prev = state[...]
    scores = _dot_nt(qb, kb) * dmask_ref[...]
    o = _dot(scores.astype(BF16), vb) + _dot(qx, s_prev.astype(BF16))
    s_new = gc_ref[0:1, :] * s_prev + _dot_tn(kz, vb)
    state[...] = s_new

    @pl.when(c == pl.num_programs(2) - 1)
    def _():
        sfin_ref[...] = s_new

    ms = jnp.mean(o * o, axis=-1, keepdims=True)
    on = o * lax.rsqrt(ms + RMS_EPS) * gnw_ref[...]
    o_ref[...] = (on * _silu(g_ref[...].astype(F32))).astype(o_ref.dtype)


def _retention_tables(chunk):
    log_gamma = jnp.log1p(-jnp.exp2(-5.0 - jnp.arange(RET_HEADS, dtype=F32)))
    idx = jnp.arange(chunk, dtype=F32)
    rel = idx[:, None] - idx[None, :]
    dmask = jnp.where((rel >= 0)[None],
                      jnp.exp(log_gamma[:, None, None] * jnp.maximum(rel, 0.0)), 0.0)
    xi = jnp.exp(log_gamma[:, None] * (idx[None, :] + 1.0))
    zeta = jnp.exp(log_gamma[:, None] * (chunk - 1.0 - idx[None, :]))
    gamma_c = jnp.exp(log_gamma * chunk)
    xi = jnp.broadcast_to(xi[:, :, None], (RET_HEADS, chunk, LANES))
    zeta = jnp.broadcast_to(zeta[:, :, None], (RET_HEADS, chunk, LANES))
    gc = jnp.broadcast_to(gamma_c[:, None, None], (RET_HEADS, SUBLANES, RET_DV))
    return dmask, xi, zeta, gc


def _rope_tables(pos):
    half = RET_DK // 2
    inv_freq = ROPE_BASE ** (-jnp.arange(half, dtype=F32) / half)
    ang = pos[:, None] * inv_freq[None, :]
    return jnp.cos(ang), jnp.sin(ang)


def retention(proj, pos, valid, gn_w, s0, chunk):
    b, s, _ = proj.shape
    nc = s // chunk
    cos, sin = _rope_tables(pos)
    dmask, xi, zeta, gc = _retention_tables(chunk)
    valid_tab = jnp.broadcast_to(valid[:, None], (s, LANES))
    kq, kv = RET_QK // RET_DK, RET_QK // RET_DV
    vg = (2 * RET_QK + RET_V) // RET_DV
    tok = lambda width, off: pl.BlockSpec((None, chunk, width), lambda bi, h, c: (bi, c, off + h))
    pos_tab = pl.BlockSpec((chunk, LANES), lambda bi, h, c: (c, 0))
    head = lambda r, w: pl.BlockSpec((None, r, w), lambda bi, h, c: (h, 0, 0))
    in_specs = [tok(RET_DK, 0), tok(RET_DK, kq), tok(RET_DV, 2 * kv), tok(RET_DV, vg),
                pos_tab, pos_tab, pos_tab,
                head(chunk, chunk), head(chunk, LANES), head(chunk, LANES), head(SUBLANES, RET_DV),
                pl.BlockSpec((1, RET_DV), lambda bi, h, c: (0, 0)),
                head(RET_DK, RET_DV)]
    out_specs = [pl.BlockSpec((None, chunk, RET_DV), lambda bi, h, c: (bi, c, h)),
                 pl.BlockSpec((None, None, RET_DK, RET_DV), lambda bi, h, c: (bi, h, 0, 0))]
    out_shape = [jax.ShapeDtypeStruct((b, s, RET_V), BF16),
                 jax.ShapeDtypeStruct((b, RET_HEADS, RET_DK, RET_DV), F32)]
    return pl.pallas_call(
        _retention_kernel, grid=(b, RET_HEADS, nc), in_specs=in_specs, out_specs=out_specs,
        out_shape=out_shape, scratch_shapes=[pltpu.VMEM((RET_DK, RET_DV), F32)],
        compiler_params=_cparams(("parallel", "parallel", "arbitrary")), name="retention")(
            proj, proj, proj, proj, cos, sin, valid_tab, dmask, xi, zeta, gc,
            gn_w.reshape(1, RET_DV), s0)


def _unit_lower_inverse(a, row, col):
    mm = functools.partial(_dot, precision=TRI_PRECISION)
    eye = (row == col).astype(F32)
    blk = lambda n: (row // n) == (col // n)
    n1 = jnp.where(blk(16), -a, 0.0)
    n2 = mm(n1, n1)
    n4 = mm(n2, n2)
    n8 = mm(n4, n4)
    t = mm(mm(mm(eye + n1, eye + n2), eye + n4), eye + n8)
    for n in (32, 64):
        off = jnp.where(blk(n) & ~blk(n // 2), a, 0.0)
        t = t - mm(mm(t, off), t)
    return t


def _dn_prep_kernel(qkv_ref, halo_ref, halo0_ref, small_ref, valid_ref, convw_ref, alog_ref,
                    dtb_ref, u_ref, w_ref, qg_ref, kd_ref, attn_ref, eg_ref, xs):
    c = pl.program_id(1)
    valid = valid_ref[...]
    vcol = valid[:, 0:1]

    @pl.when(c == 0)
    def _():
        xs[0:SUBLANES, :] = halo0_ref[...].astype(F32)

    @pl.when(c > 0)
    def _():
        xs[0:SUBLANES, :] = halo_ref[...].astype(F32)

    xs[SUBLANES:, :] = qkv_ref[...].astype(F32) * vcol

    def conv_silu(lo, width):
        acc = None
        for j in range(CONV_K):
            r0 = SUBLANES - (CONV_K - 1) + j
            term = xs[r0:r0 + CHUNK, lo:lo + width] * convw_ref[j:j + 1, lo:lo + width]
            acc = term if acc is None else acc + term
        return _silu(acc)

    small = small_ref[...]
    beta_all = jax.nn.sigmoid(small) * valid
    g_all = -jnp.exp(alog_ref[...]) * jax.nn.softplus(small + dtb_ref[...]) * valid
    row = lax.broadcasted_iota(jnp.int32, (CHUNK, CHUNK), 0)
    col = lax.broadcasted_iota(jnp.int32, (CHUNK, CHUNK), 1)
    incl = row >= col
    strict = row > col
    gam_all = _dot(incl.astype(F32), g_all, precision=lax.Precision.HIGHEST)
    gam_rows = _dot(g_all.T, (row <= col).astype(F32), precision=lax.Precision.HIGHEST)

    attn_parts = []
    eg_parts = []
    for h in range(DN_HEADS):
        q = conv_silu(h * DN_DK, DN_DK)
        k = conv_silu(DN_QK + h * DN_DK, DN_DK)
        v = conv_silu(2 * DN_QK + h * DN_DV, DN_DV)
        q = q * lax.rsqrt(jnp.sum(q * q, axis=-1, keepdims=True) + RMS_EPS) * (DN_DK ** -0.5)
        k = k * lax.rsqrt(jnp.sum(k * k, axis=-1, keepdims=True) + RMS_EPS)
        beta = beta_all[:, h:h + 1]
        gcol = gam_all[:, DN_HEADS + h:DN_HEADS + h + 1]
        grow = gam_rows[DN_HEADS + h:DN_HEADS + h + 1, :]
        glast = gcol[CHUNK - 1:CHUNK, :]
        decay = jnp.exp(jnp.where(incl, gcol - grow, -jnp.inf))
        kb = k.astype(BF16)
        a_mat = jnp.where(strict, beta * _dot_nt(kb, kb) * decay, 0.0)
        t = _unit_lower_inverse(a_mat, row, col)
        egam = jnp.exp(gcol)
        u = _dot(t, v * beta, precision=TRI_PRECISION)
        w = _dot(t, k * beta * egam, precision=TRI_PRECISION)
        u_ref[:, h * DN_DV:(h + 1) * DN_DV] = u
        w_ref[:, h * DN_DK:(h + 1) * DN_DK] = w.astype(BF16)
        qg_ref[:, h * DN_DK:(h + 1) * DN_DK] = (q * egam).astype(BF16)
        kd_ref[:, h * DN_DK:(h + 1) * DN_DK] = (k * jnp.exp(glast - gcol)).astype(BF16)
        attn_parts.append((_dot_nt(q.astype(BF16), kb) * decay).astype(BF16))
        eg_parts.append(jnp.broadcast_to(jnp.exp(glast), (1, DN_DV)))
    attn_ref[...] = jnp.concatenate(attn_parts, axis=-1)
    eg_ref[...] = jnp.concatenate(eg_parts, axis=0)


def dn_prep(proj, small, valid, halo0, conv_w, a_log, dt_bias):
    b, s, n_proj = proj.shape
    nc = s // CHUNK
    hb = CHUNK // SUBLANES
    valid_tab = jnp.broadcast_to(valid[:, None], (s, LANES))
    pad_heads = lambda v: jnp.zeros((1, LANES), F32).at[0, DN_HEADS:2 * DN_HEADS].set(v)
    tok = lambda width: pl.BlockSpec((None, CHUNK, width), lambda bi, c: (bi, c, 0))
    in_specs = [tok(DN_CONV_CH),
                pl.BlockSpec((None, SUBLANES, DN_CONV_CH), lambda bi, c: (bi, jnp.maximum(c * hb - 1, 0), 0)),
                pl.BlockSpec((SUBLANES, DN_CONV_CH), lambda bi, c: (0, 0)),
                tok(LANES),
                pl.BlockSpec((CHUNK, LANES), lambda bi, c: (c, 0)),
                pl.BlockSpec((CONV_K, DN_CONV_CH), lambda bi, c: (0, 0)),
                pl.BlockSpec((1, LANES), lambda bi, c: (0, 0)),
                pl.BlockSpec((1, LANES), lambda bi, c: (0, 0))]
    out_specs = [tok(DN_V), tok(DN_QK), tok(DN_QK), tok(DN_QK), tok(DN_HEADS * CHUNK),
                 pl.BlockSpec((None, DN_HEADS, DN_DV), lambda bi, c: (bi, c, 0))]
    out_shape = [jax.ShapeDtypeStruct((b, s, DN_V), F32),
                 jax.ShapeDtypeStruct((b, s, DN_QK), BF16),
                 jax.ShapeDtypeStruct((b, s, DN_QK), BF16),
                 jax.ShapeDtypeStruct((b, s, DN_QK), BF16),
                 jax.ShapeDtypeStruct((b, s, DN_HEADS * CHUNK), BF16),
                 jax.ShapeDtypeStruct((b, nc * DN_HEADS, DN_DV), F32)]
    return pl.pallas_call(
        _dn_prep_kernel, grid=(b, nc), in_specs=in_specs, out_specs=out_specs, out_shape=out_shape,
        scratch_shapes=[pltpu.VMEM((CHUNK + SUBLANES, DN_CONV_CH), F32)],
        compiler_params=_cparams(("parallel", "arbitrary")), name="dn_prep")(
            proj, proj, halo0, small, valid_tab, conv_w, pad_heads(a_log), pad_heads(dt_bias))


def _dn_rec_kernel(u_ref, w_ref, qg_ref, kd_ref, attn_ref, eg_ref, gate_ref, nw_ref, s0_ref,
                   o_ref, sfin_ref, state):
    c = pl.program_id(1)

    @pl.when(c == 0)
    def _():
        state[...] = s0_ref[...]

    for h in range(DN_HEADS):
        dk = slice(h * DN_DK, (h + 1) * DN_DK)
        dv = slice(h * DN_DV, (h + 1) * DN_DV)
        s_prev = state[h]
        wq = jnp.concatenate([w_ref[:, dk], qg_ref[:, dk]], axis=0)
        ws = _dot(wq, s_prev.astype(BF16))
        v_new = u_ref[:, dv] - ws[:CHUNK]
        vb = v_new.astype(BF16)
        o = ws[CHUNK:] + _dot(attn_ref[:, h * CHUNK:(h + 1) * CHUNK], vb)
        s_new = s_prev * eg_ref[h:h + 1, :] + _dot_tn(kd_ref[:, dk], vb)
        state[h] = s_new
        ms = jnp.mean(o * o, axis=-1, keepdims=True)
        on = o * lax.rsqrt(ms + RMS_EPS) * nw_ref[...]
        o_ref[:, dv] = (on * _silu(gate_ref[:, dv].astype(F32))).astype(o_ref.dtype)

    @pl.when(c == pl.num_programs(1) - 1)
    def _():
        sfin_ref[...] = state[...]


def dn_rec(u, w, qg, kd, attn, eg, proj, norm_w, s0):
    b, s, _ = u.shape
    nc = s // CHUNK
    gate_blk = DN_CONV_CH // DN_V
    tok = lambda width, off=0: pl.BlockSpec((None, CHUNK, width), lambda bi, c: (bi, c, off))
    in_specs = [tok(DN_V), tok(DN_QK), tok(DN_QK), tok(DN_QK), tok(DN_HEADS * CHUNK),
                pl.BlockSpec((None, DN_HEADS, DN_DV), lambda bi, c: (bi, c, 0)),
                tok(DN_V, gate_blk),
                pl.BlockSpec((1, DN_DV), lambda bi, c: (0, 0)),
                pl.BlockSpec((DN_HEADS, DN_DK, DN_DV), lambda bi, c: (0, 0, 0))]
    out_specs = [tok(DN_V),
                 pl.BlockSpec((None, DN_HEADS, DN_DK, DN_DV), lambda bi, c: (bi, 0, 0, 0))]
    out_shape = [jax.ShapeDtypeStruct((b, s, DN_V), BF16),
                 jax.ShapeDtypeStruct((b, DN_HEADS, DN_DK, DN_DV), F32)]
    return pl.pallas_call(
        _dn_rec_kernel, grid=(b, nc), in_specs=in_specs, out_specs=out_specs, out_shape=out_shape,
        scratch_shapes=[pltpu.VMEM((DN_HEADS, DN_DK, DN_DV), F32)],
        compiler_params=_cparams(("parallel", "arbitrary")), name="dn_rec")(
            u, w, qg, kd, attn, eg, proj, norm_w.reshape(1, DN_DV), s0)


def _layer_retention(h, pos, valid, chunk, s0, p):
    b, s, d = h.shape
    h2 = h.reshape(b * s, d)
    proj = rms_inproj(h2, p["mix_norm_w"], p["w_in"]).reshape(b, s, -1)
    o, s_fin = retention(proj, pos, valid, p["gn_w"], s0, chunk)
    h2 = out_ffn(h2, o.reshape(b * s, RET_V), p["w_out"], p["ffn_norm_w"], p["w_gate"], p["w_up"],
                 p["w_down"])
    return h2.reshape(b, s, d), s_fin


def _layer_deltanet(h, valid, s0, halo0, p, final_w=None, run_tail=True):
    b, s, d = h.shape
    h2 = h.reshape(b * s, d)
    proj, small = rms_inproj(h2, p["mix_norm_w"], p["w_in"], p["w_in_small"])
    proj = proj.reshape(b, s, -1)
    small = small.reshape(b, s, -1)
    u, w, qg, kd, attn, eg = dn_prep(proj, small, valid, halo0, p["conv_w"], p["a_log"], p["dt_bias"])
    o, s_fin = dn_rec(u, w, qg, kd, attn, eg, proj, p["norm_w"], s0)
    out = None
    if run_tail:
        out = out_ffn(h2, o.reshape(b * s, DN_V), p["w_out"], p["ffn_norm_w"], p["w_gate"], p["w_up"],
                      p["w_down"], final_w).reshape(b, s, d)
    return out, s_fin, proj


def kernel(x, meta_tokens, mix_norm_w, ffn_norm_w, ret_w_in, ret_gn_w, ret_w_out, dn_w_in, dn_conv_w,
           dn_a_log, dn_dt_bias, dn_norm_w, dn_w_out, ffn_w_gate, ffn_w_up, ffn_w_down, final_norm_w):
    b, s, d = x.shape
    bf = lambda t: t.astype(BF16)
    dn_main = DN_CONV_CH + DN_V
    small_w = jnp.zeros((d, LANES), F32).at[:, :2 * DN_HEADS].set(dn_w_in[0][:, dn_main:])
    p_ret = dict(mix_norm_w=mix_norm_w[0], w_in=bf(ret_w_in[0]), gn_w=ret_gn_w[0], w_out=bf(ret_w_out[0]),
                 ffn_norm_w=ffn_norm_w[0], w_gate=bf(ffn_w_gate[0]), w_up=bf(ffn_w_up[0]),
                 w_down=bf(ffn_w_down[0]))
    p_dn = dict(mix_norm_w=mix_norm_w[1], w_in=bf(dn_w_in[0][:, :dn_main]), w_in_small=bf(small_w),
                conv_w=dn_conv_w[0], a_log=dn_a_log[0], dt_bias=dn_dt_bias[0], norm_w=dn_norm_w[0],
                w_out=bf(dn_w_out[0]), ffn_norm_w=ffn_norm_w[1], w_gate=bf(ffn_w_gate[1]),
                w_up=bf(ffn_w_up[1]), w_down=bf(ffn_w_down[1]))

    h_meta = jnp.concatenate([jnp.zeros((PAD, d), x.dtype), meta_tokens.astype(x.dtype)], axis=0)[None]
    pos_meta = (jnp.arange(CHUNK) - PAD).astype(F32)
    valid_meta = (pos_meta >= 0).astype(F32)
    ret_s0 = jnp.zeros((RET_HEADS, RET_DK, RET_DV), F32)
    dn_s0 = jnp.zeros((DN_HEADS, DN_DK, DN_DV), F32)
    halo_zero = jnp.zeros((SUBLANES, DN_CONV_CH), BF16)
    h_meta, ret_state = _layer_retention(h_meta, pos_meta, valid_meta, CHUNK, ret_s0, p_ret)
    _, dn_state, proj_meta = _layer_deltanet(h_meta, valid_meta, dn_s0, halo_zero, p_dn, run_tail=False)
    halo_meta = proj_meta[0, CHUNK - SUBLANES:, :DN_CONV_CH]

    pos = (jnp.arange(s) + N_META).astype(F32)
    valid = jnp.ones((s,), F32)
    h, _ = _layer_retention(x, pos, valid, RET_CHUNK, ret_state[0], p_ret)
    out, _, _ = _layer_deltanet(h, valid, dn_state[0], halo_meta, p_dn, final_w=final_norm_w)
    return out
```

```python
import functools
import math

import jax
import jax.numpy as jnp
from jax import lax
from jax.experimental import pallas as pl
from jax.experimental.pallas import tpu as pltpu

F32 = jnp.float32
BF16 = jnp.bfloat16

N_META = 16
CHUNK = 64
PAD = CHUNK - N_META
RMS_EPS = 1e-6
RET_HEADS = 4
RET_DK = 256
RET_DV = 512
RET_QK = RET_HEADS * RET_DK
RET_V = RET_HEADS * RET_DV
ROPE_BASE = 10000.0
DN_HEADS = 8
DN_DK = 128
DN_DV = 256
DN_QK = DN_HEADS * DN_DK
DN_V = DN_HEADS * DN_DV
DN_CONV_CH = 2 * DN_QK + DN_V
CONV_K = 4

LANES = 128
SUBLANES = 8
VMEM_LIMIT_BYTES = 56 * 1024 * 1024
TOKEN_BLOCK = 512
PROJ_N_CHUNK = 1024
RET_CHUNK = 256


def _cparams(sem):
    return pltpu.CompilerParams(dimension_semantics=sem, vmem_limit_bytes=VMEM_LIMIT_BYTES)


def _resident(shape):
    nd = len(shape)
    return pl.BlockSpec(shape, lambda *_: (0,) * nd, pipeline_mode=pl.Buffered(1))


def _rms(x, w):
    ms = jnp.mean(x * x, axis=-1, keepdims=True)
    return x * lax.rsqrt(ms + RMS_EPS) * w


def _silu(x):
    return x * jax.nn.sigmoid(x)


def _dot(a, b, precision=None):
    return jnp.dot(a, b, preferred_element_type=F32, precision=precision)


def _dot_nt(a, b):
    return lax.dot_general(a, b, (((1,), (1,)), ((), ())), preferred_element_type=F32)


def _dot_tn(a, b):
    return lax.dot_general(a, b, (((0,), (0,)), ((), ())), preferred_element_type=F32)


def _rms_inproj_kernel(*refs, has_small):
    if has_small:
        h_ref, nw_ref, w_ref, ws_ref, o_ref, os_ref = refs
    else:
        h_ref, nw_ref, w_ref, o_ref = refs
    xn = _rms(h_ref[...], nw_ref[...]).astype(BF16)
    n_total = o_ref.shape[-1]
    for n0 in range(0, n_total, PROJ_N_CHUNK):
        n1 = min(n0 + PROJ_N_CHUNK, n_total)
        o_ref[:, n0:n1] = _dot(xn, w_ref[:, n0:n1]).astype(o_ref.dtype)
    if has_small:
        os_ref[...] = _dot(xn, ws_ref[...])


def rms_inproj(h2d, norm_w, w, w_small=None):
    m, d = h2d.shape
    n = w.shape[1]
    bm = min(TOKEN_BLOCK, m)
    has_small = w_small is not None
    in_specs = [pl.BlockSpec((bm, d), lambda i: (i, 0)), _resident((1, d)), _resident((d, n))]
    args = [h2d, norm_w.reshape(1, d), w]
    out_shape = [jax.ShapeDtypeStruct((m, n), BF16)]
    out_specs = [pl.BlockSpec((bm, n), lambda i: (i, 0))]
    if has_small:
        ns = w_small.shape[1]
        in_specs.append(_resident((d, ns)))
        args.append(w_small)
        out_shape.append(jax.ShapeDtypeStruct((m, ns), F32))
        out_specs.append(pl.BlockSpec((bm, ns), lambda i: (i, 0)))
    outs = pl.pallas_call(
        functools.partial(_rms_inproj_kernel, has_small=has_small),
        grid=(m // bm,), in_specs=in_specs, out_specs=out_specs, out_shape=out_shape,
        compiler_params=_cparams(("parallel",)), name="rms_inproj")(*args)
    return outs if has_small else outs[0]


def _out_ffn_kernel(*refs, final):
    if final:
        h_ref, o_ref, wo_ref, nw_ref, wg_ref, wu_ref, wd_ref, fw_ref, out_ref = refs
    else:
        h_ref, o_ref, wo_ref, nw_ref, wg_ref, wu_ref, wd_ref, out_ref = refs
    h1 = h_ref[...] + _dot(o_ref[...], wo_ref[...])
    xn = _rms(h1, nw_ref[...]).astype(BF16)
    act = (_silu(_dot(xn, wg_ref[...])) * _dot(xn, wu_ref[...])).astype(BF16)
    h2 = h1 + _dot(act, wd_ref[...])
    if final:
        h2 = _rms(h2, fw_ref[...])
    out_ref[...] = h2


def out_ffn(h2d, o2d, w_out, norm_w, w_gate, w_up, w_down, final_w=None):
    m, d = h2d.shape
    dv = o2d.shape[1]
    f = w_gate.shape[1]
    bm = min(TOKEN_BLOCK, m)
    final = final_w is not None
    row = lambda width: pl.BlockSpec((bm, width), lambda i: (i, 0))
    in_specs = [row(d), row(dv), _resident((dv, d)), _resident((1, d)),
                _resident((d, f)), _resident((d, f)), _resident((f, d))]
    args = [h2d, o2d, w_out, norm_w.reshape(1, d), w_gate, w_up, w_down]
    if final:
        in_specs.append(_resident((1, d)))
        args.append(final_w.reshape(1, d))
    return pl.pallas_call(
        functools.partial(_out_ffn_kernel, final=final),
        grid=(m // bm,), in_specs=in_specs, out_specs=row(d),
        out_shape=jax.ShapeDtypeStruct((m, d), F32),
        compiler_params=_cparams(("parallel",)), name="out_ffn")(*args)


def _retention_kernel(q_ref, k_ref, v_ref, g_ref, cos_ref, sin_ref, valid_ref, dmask_ref,
                      xi_ref, zeta_ref, gc_ref, gnw_ref, s0_ref, o_ref, sfin_ref, state):
    c = pl.program_id(2)

    @pl.when(c == 0)
    def _():
        state[...] = s0_ref[...]

    cos = cos_ref[...]
    sin = sin_ref[...]
    half = RET_DK // 2

    def rope(t_ref):
        t = t_ref[...].astype(F32)
        t1, t2 = t[:, :half], t[:, half:]
        return t1 * cos - t2 * sin, t1 * sin + t2 * cos

    q1, q2 = rope(q_ref)
    k1, k2 = rope(k_ref)
    valid = valid_ref[...]
    k1 = k1 * (RET_DK ** -0.5) * valid
    k2 = k2 * (RET_DK ** -0.5) * valid
    xi = xi_ref[...]
    zeta = zeta_ref[...]
    qb = jnp.concatenate([q1, q2], axis=-1).astype(BF16)
    kb = jnp.concatenate([k1, k2], axis=-1).astype(BF16)
    qx = jnp.concatenate([q1 * xi, q2 * xi], axis=-1).astype(BF16)
    kz = jnp.concatenate([k1 * zeta, k2 * zeta], axis=-1).astype(BF16)
    vb = v_ref[...]

    s_prev = state[...]
    scores = _dot_nt(qb, kb) * dmask_ref[...]
    o = _dot(scores.astype(BF16), vb) + _dot(qx, s_prev.astype(BF16))
    s_new = gc_ref[0:1, :] * s_prev + _dot_tn(kz, vb)
    state[...] = s_new

    @pl.when(c == pl.num_programs(2) - 1)
    def _():
        sfin_ref[...] = s_new

    ms = jnp.mean(o * o, axis=-1, keepdims=True)
    on = o * lax.rsqrt(ms + RMS_EPS) * gnw_ref[...]
    o_ref[...] = (on * _silu(g_ref[...].astype(F32))).astype(o_ref.dtype)


def _retention_tables(chunk):
    log_gamma = jnp.log1p(-jnp.exp2(-5.0 - jnp.arange(RET_HEADS, dtype=F32)))
    idx = jnp.arange(chunk, dtype=F32)
    rel = idx[:, None] - idx[None, :]
    dmask = jnp.where((rel >= 0)[None],
                      jnp.exp(log_gamma[:, None, None] * jnp.maximum(rel, 0.0)), 0.0)
    xi = jnp.exp(log_gamma[:, None] * (idx[None, :] + 1.0))
    zeta = jnp.exp(log_gamma[:, None] * (chunk - 1.0 - idx[None, :]))
    gamma_c = jnp.exp(log_gamma * chunk)
    xi = jnp.broadcast_to(xi[:, :, None], (RET_HEADS, chunk, LANES))
    zeta = jnp.broadcast_to(zeta[:, :, None], (RET_HEADS, chunk, LANES))
    gc = jnp.broadcast_to(gamma_c[:, None, None], (RET_HEADS, SUBLANES, RET_DV))
    return dmask, xi, zeta, gc


def _rope_tables(pos):
    half = RET_DK // 2
    inv_freq = ROPE_BASE ** (-jnp.arange(half, dtype=F32) / half)
    ang = pos[:, None] * inv_freq[None, :]
    return jnp.cos(ang), jnp.sin(ang)


def retention(proj, pos, valid, gn_w, s0, chunk):
    b, s, _ = proj.shape
    nc = s // chunk
    cos, sin = _rope_tables(pos)
    dmask, xi, zeta, gc = _retention_tables(chunk)
    valid_tab = jnp.broadcast_to(valid[:, None], (s, LANES))
    kq, kv = RET_QK // RET_DK, RET_QK // RET_DV
    vg = (2 * RET_QK + RET_V) // RET_DV
    tok = lambda width, off: pl.BlockSpec((None, chunk, width), lambda bi, h, c: (bi, c, off + h))
    pos_tab = pl.BlockSpec((chunk, LANES), lambda bi, h, c: (c, 0))
    head = lambda r, w: pl.BlockSpec((None, r, w), lambda bi, h, c: (h, 0, 0))
    in_specs = [tok(RET_DK, 0), tok(RET_DK, kq), tok(RET_DV, 2 * kv), tok(RET_DV, vg),
                pos_tab, pos_tab, pos_tab,
                head(chunk, chunk), head(chunk, LANES), head(chunk, LANES), head(SUBLANES, RET_DV),
                pl.BlockSpec((1, RET_DV), lambda bi, h, c: (0, 0)),
                head(RET_DK, RET_DV)]
    out_specs = [pl.BlockSpec((None, chunk, RET_DV), lambda bi, h, c: (bi, c, h)),
                 pl.BlockSpec((None, None, RET_DK, RET_DV), lambda bi, h, c: (bi, h, 0, 0))]
    out_shape = [jax.ShapeDtypeStruct((b, s, RET_V), BF16),
                 jax.ShapeDtypeStruct((b, RET_HEADS, RET_DK, RET_DV), F32)]
    return pl.pallas_call(
        _retention_kernel, grid=(b, RET_HEADS, nc), in_specs=in_specs, out_specs=out_specs,
        out_shape=out_shape, scratch_shapes=[pltpu.VMEM((RET_DK, RET_DV), F32)],
        compiler_params=_cparams(("parallel", "parallel", "arbitrary")), name="retention")(
            proj, proj, proj, proj, cos, sin, valid_tab, dmask, xi, zeta, gc,
            gn_w.reshape(1, RET_DV), s0)


def _split(x):
    hi = x.astype(BF16)
    lo = (x - hi.astype(F32)).astype(BF16)
    return hi, lo


def _mm_split(xs, ys):
    (xh, xl), (yh, yl) = xs, ys
    lhs = jnp.concatenate([xh, xh, xl], axis=1)
    rhs = jnp.concatenate([yh, yl, yh], axis=0)
    return _dot(lhs, rhs)


def _unit_lower_inverses(a_list, row, col):
    eye = (row == col).astype(F32)
    blk = lambda n: (row // n) == (col // n)
    sq = lambda xs: [_mm_split(x, x) for x in xs]
    n1 = [jnp.where(blk(16), -a, 0.0) for a in a_list]
    n2 = sq([_split(x) for x in n1])
    n4 = sq([_split(x) for x in n2])
    n8 = sq([_split(x) for x in n4])
    t = [_mm_split(_split(eye + a), _split(eye + b)) for a, b in zip(n1, n2)]
    t = [_mm_split(_split(x), _split(eye + y)) for x, y in zip(t, n4)]
    t = [_mm_split(_split(x), _split(eye + y)) for x, y in zip(t, n8)]
    for n in (32, 64):
        off_mask = blk(n) & ~blk(n // 2)
        ts = [_split(x) for x in t]
        to = [_mm_split(x, _split(jnp.where(off_mask, a, 0.0))) for x, a in zip(ts, a_list)]
        t = [x - _mm_split(_split(y), xs) for x, y, xs in zip(t, to, ts)]
    return [_split(x) for x in t]


def _dn_prep_kernel(qkv_ref, halo_ref, halo0_ref, small_ref, valid_ref, convw_ref, alog_ref,
                    dtb_ref, u_ref, w_ref, qg_ref, kd_ref, attn_ref, eg_ref, xs):
    c = pl.program_id(1)
    valid = valid_ref[...]
    vcol = valid[:, 0:1]

    @pl.when(c == 0)
    def _():
        xs[0:SUBLANES, :] = halo0_ref[...].astype(F32)

    @pl.when(c > 0)
    def _():
        xs[0:SUBLANES, :] = halo_ref[...].astype(F32)

    xs[SUBLANES:, :] = qkv_ref[...].astype(F32) * vcol

    def conv_silu(lo, width):
        acc = None
        for j in range(CONV_K):
            r0 = SUBLANES - (CONV_K - 1) + j
            term = xs[r0:r0 + CHUNK, lo:lo + width] * convw_ref[j:j + 1, lo:lo + width]
            acc = term if acc is None else acc + term
        return _silu(acc)

    small = small_ref[...]
    beta_all = jax.nn.sigmoid(small) * valid
    g_all = -jnp.exp(alog_ref[...]) * jax.nn.softplus(small + dtb_ref[...]) * valid
    row = lax.broadcasted_iota(jnp.int32, (CHUNK, CHUNK), 0)
    col = lax.broadcasted_iota(jnp.int32, (CHUNK, CHUNK), 1)
    incl = row >= col
    strict = row > col

    def split3(x):
        hi, lo = _split(x)
        lo2 = (x - hi.astype(F32) - lo.astype(F32)).astype(BF16)
        return hi, lo, lo2

    tri_l = incl.astype(BF16)
    tri_u = (row <= col).astype(BF16)
    gam_all = _dot(jnp.concatenate([tri_l] * 3, axis=1), jnp.concatenate(split3(g_all), axis=0))
    g_rows = g_all.T[DN_HEADS:2 * DN_HEADS, :]
    gam_rows = _dot(jnp.concatenate(split3(g_rows), axis=1), jnp.concatenate([tri_u] * 3, axis=0))

    heads = range(DN_HEADS)
    q = [conv_silu(h * DN_DK, DN_DK) for h in heads]
    k = [conv_silu(DN_QK + h * DN_DK, DN_DK) for h in heads]
    v = [conv_silu(2 * DN_QK + h * DN_DV, DN_DV) for h in heads]
    q = [x * lax.rsqrt(jnp.sum(x * x, axis=-1, keepdims=True) + RMS_EPS) * (DN_DK ** -0.5) for x in q]
    k = [x * lax.rsqrt(jnp.sum(x * x, axis=-1, keepdims=True) + RMS_EPS) for x in k]
    beta = [beta_all[:, h:h + 1] for h in heads]
    gcol = [gam_all[:, DN_HEADS + h:DN_HEADS + h + 1] for h in heads]
    grow = [gam_rows[h:h + 1, :] for h in heads]
    glast = [x[CHUNK - 1:CHUNK, :] for x in gcol]
    decay = [jnp.exp(jnp.where(incl, gc - gr, -jnp.inf)) for gc, gr in zip(gcol, grow)]
    kb = [x.astype(BF16) for x in k]
    kq = [_dot_nt(jnp.concatenate([kbh, qh.astype(BF16)], axis=0), kbh) for kbh, qh in zip(kb, q)]
    a_mat = [jnp.where(strict, b * x[:CHUNK] * d, 0.0) for b, x, d in zip(beta, kq, decay)]
    t = _unit_lower_inverses(a_mat, row, col)
    egam = [jnp.exp(x) for x in gcol]
    rhs = [jnp.concatenate([vh * b, kh * b * e], axis=1) for vh, kh, b, e in zip(v, k, beta, egam)]
    uw = [_mm_split(th, _split(r)) for th, r in zip(t, rhs)]
    u_ref[...] = jnp.concatenate([x[:, :DN_DV] for x in uw], axis=1)
    w_ref[...] = jnp.concatenate([x[:, DN_DV:] for x in uw], axis=1).astype(BF16)
    qg_ref[...] = jnp.concatenate([x * e for x, e in zip(q, egam)], axis=1).astype(BF16)
    kd_ref[...] = jnp.concatenate([x * jnp.exp(gl - gc) for x, gl, gc in zip(k, glast, gcol)],
                                  axis=1).astype(BF16)
    attn_ref[...] = jnp.concatenate([x[CHUNK:] * d for x, d in zip(kq, decay)], axis=1).astype(BF16)
    eg_ref[...] = jnp.concatenate([jnp.broadcast_to(jnp.exp(x), (1, DN_DV)) for x in glast], axis=0)


def dn_prep(proj, small, valid, halo0, conv_w, a_log, dt_bias):
    b, s, n_proj = proj.shape
    nc = s // CHUNK
    hb = CHUNK // SUBLANES
    valid_tab = jnp.broadcast_to(valid[:, None], (s, LANES))
    pad_heads = lambda v: jnp.zeros((1, LANES), F32).at[0, DN_HEADS:2 * DN_HEADS].set(v)
    tok = lambda width: pl.BlockSpec((None, CHUNK, width), lambda bi, c: (bi, c, 0))
    in_specs = [tok(DN_CONV_CH),
                pl.BlockSpec((None, SUBLANES, DN_CONV_CH), lambda bi, c: (bi, jnp.maximum(c * hb - 1, 0), 0)),
                pl.BlockSpec((SUBLANES, DN_CONV_CH), lambda bi, c: (0, 0)),
                tok(LANES),
                pl.BlockSpec((CHUNK, LANES), lambda bi, c: (c, 0)),
                pl.BlockSpec((CONV_K, DN_CONV_CH), lambda bi, c: (0, 0)),
                pl.BlockSpec((1, LANES), lambda bi, c: (0, 0)),
                pl.BlockSpec((1, LANES), lambda bi, c: (0, 0))]
    out_specs = [tok(DN_V), tok(DN_QK), tok(DN_QK), tok(DN_QK), tok(DN_HEADS * CHUNK),
                 pl.BlockSpec((None, DN_HEADS, DN_DV), lambda bi, c: (bi, c, 0))]
    out_shape = [jax.ShapeDtypeStruct((b, s, DN_V), F32),
                 jax.ShapeDtypeStruct((b, s, DN_QK), BF16),
                 jax.ShapeDtypeStruct((b, s, DN_QK), BF16),
                 jax.ShapeDtypeStruct((b, s, DN_QK), BF16),
                 jax.ShapeDtypeStruct((b, s, DN_HEADS * CHUNK), BF16),
                 jax.ShapeDtypeStruct((b, nc * DN_HEADS, DN_DV), F32)]
    return pl.pallas_call(
        _dn_prep_kernel, grid=(b, nc), in_specs=in_specs, out_specs=out_specs, out_shape=out_shape,
        scratch_shapes=[pltpu.VMEM((CHUNK + SUBLANES, DN_CONV_CH), F32)],
        compiler_params=_cparams(("parallel", "arbitrary")), name="dn_prep")(
            proj, proj, halo0, small, valid_tab, conv_w, pad_heads(a_log), pad_heads(dt_bias))


def _dn_rec_kernel(u_ref, w_ref, qg_ref, kd_ref, attn_ref, eg_ref, gate_ref, nw_ref, s0_ref,
                   o_ref, sfin_ref, state):
    c = pl.program_id(1)

    @pl.when(c == 0)
    def _():
        state[...] = s0_ref[...]

    heads = range(DN_HEADS)
    dk = [slice(h * DN_DK, (h + 1) * DN_DK) for h in heads]
    dv = [slice(h * DN_DV, (h + 1) * DN_DV) for h in heads]
    s_prev = [state[h] for h in heads]
    wq = [jnp.concatenate([w_ref[:, dk[h]], qg_ref[:, dk[h]]], axis=0) for h in heads]
    ws = [_dot(wq[h], s_prev[h].astype(BF16)) for h in heads]
    vb = [(u_ref[:, dv[h]] - ws[h][:CHUNK]).astype(BF16) for h in heads]
    o = [ws[h][CHUNK:] + _dot(attn_ref[:, h * CHUNK:(h + 1) * CHUNK], vb[h]) for h in heads]
    s_new = [s_prev[h] * eg_ref[h:h + 1, :] + _dot_tn(kd_ref[:, dk[h]], vb[h]) for h in heads]
    state[...] = jnp.stack(s_new, axis=0)
    nw = nw_ref[...]
    on = [x * lax.rsqrt(jnp.mean(x * x, axis=-1, keepdims=True) + RMS_EPS) * nw for x in o]
    o_ref[...] = (jnp.concatenate(on, axis=1) * _silu(gate_ref[...].astype(F32))).astype(o_ref.dtype)

    @pl.when(c == pl.num_programs(1) - 1)
    def _():
        sfin_ref[...] = state[...]


def dn_rec(u, w, qg, kd, attn, eg, proj, norm_w, s0):
    b, s, _ = u.shape
    nc = s // CHUNK
    gate_blk = DN_CONV_CH // DN_V
    tok = lambda width, off=0: pl.BlockSpec((None, CHUNK, width), lambda bi, c: (bi, c, off))
    in_specs = [tok(DN_V), tok(DN_QK), tok(DN_QK), tok(DN_QK), tok(DN_HEADS * CHUNK),
                pl.BlockSpec((None, DN_HEADS, DN_DV), lambda bi, c: (bi, c, 0)),
                tok(DN_V, gate_blk),
                pl.BlockSpec((1, DN_DV), lambda bi, c: (0, 0)),
                pl.BlockSpec((DN_HEADS, DN_DK, DN_DV), lambda bi, c: (0, 0, 0))]
    out_specs = [tok(DN_V),
                 pl.BlockSpec((None, DN_HEADS, DN_DK, DN_DV), lambda bi, c: (bi, 0, 0, 0))]
    out_shape = [jax.ShapeDtypeStruct((b, s, DN_V), BF16),
                 jax.ShapeDtypeStruct((b, DN_HEADS, DN_DK, DN_DV), F32)]
    return pl.pallas_call(
        _dn_rec_kernel, grid=(b, nc), in_specs=in_specs, out_specs=out_specs, out_shape=out_shape,
        scratch_shapes=[pltpu.VMEM((DN_HEADS, DN_DK, DN_DV), F32)],
        compiler_params=_cparams(("parallel", "arbitrary")), name="dn_rec")(
            u, w, qg, kd, attn, eg, proj, norm_w.reshape(1, DN_DV), s0)


def _layer_retention(h, pos, valid, chunk, s0, p):
    b, s, d = h.shape
    h2 = h.reshape(b * s, d)
    proj = rms_inproj(h2, p["mix_norm_w"], p["w_in"]).reshape(b, s, -1)
    o, s_fin = retention(proj, pos, valid, p["gn_w"], s0, chunk)
    h2 = out_ffn(h2, o.reshape(b * s, RET_V), p["w_out"], p["ffn_norm_w"], p["w_gate"], p["w_up"],
                 p["w_down"])
    return h2.reshape(b, s, d), s_fin


def _layer_deltanet(h, valid, s0, halo0, p, final_w=None, run_tail=True):
    b, s, d = h.shape
    h2 = h.reshape(b * s, d)
    proj, small = rms_inproj(h2, p["mix_norm_w"], p["w_in"], p["w_in_small"])
    proj = proj.reshape(b, s, -1)
    small = small.reshape(b, s, -1)
    u, w, qg, kd, attn, eg = dn_prep(proj, small, valid, halo0, p["conv_w"], p["a_log"], p["dt_bias"])
    o, s_fin = dn_rec(u, w, qg, kd, attn, eg, proj, p["norm_w"], s0)
    out = None
    if run_tail:
        out = out_ffn(h2, o.reshape(b * s, DN_V), p["w_out"], p["ffn_norm_w"], p["w_gate"], p["w_up"],
                      p["w_down"], final_w).reshape(b, s, d)
    return out, s_fin, proj


def kernel(x, meta_tokens, mix_norm_w, ffn_norm_w, ret_w_in, ret_gn_w, ret_w_out, dn_w_in, dn_conv_w,
           dn_a_log, dn_dt_bias, dn_norm_w, dn_w_out, ffn_w_gate, ffn_w_up, ffn_w_down, final_norm_w):
    b, s, d = x.shape
    bf = lambda t: t.astype(BF16)
    dn_main = DN_CONV_CH + DN_V
    small_w = jnp.zeros((d, LANES), F32).at[:, :2 * DN_HEADS].set(dn_w_in[0][:, dn_main:])
    p_ret = dict(mix_norm_w=mix_norm_w[0], w_in=bf(ret_w_in[0]), gn_w=ret_gn_w[0], w_out=bf(ret_w_out[0]),
                 ffn_norm_w=ffn_norm_w[0], w_gate=bf(ffn_w_gate[0]), w_up=bf(ffn_w_up[0]),
                 w_down=bf(ffn_w_down[0]))
    p_dn = dict(mix_norm_w=mix_norm_w[1], w_in=bf(dn_w_in[0][:, :dn_main]), w_in_small=bf(small_w),
                conv_w=dn_conv_w[0], a_log=dn_a_log[0], dt_bias=dn_dt_bias[0], norm_w=dn_norm_w[0],
                w_out=bf(dn_w_out[0]), ffn_norm_w=ffn_norm_w[1], w_gate=bf(ffn_w_gate[1]),
                w_up=bf(ffn_w_up[1]), w_down=bf(ffn_w_down[1]))

    h_meta = jnp.concatenate([jnp.zeros((PAD, d), x.dtype), meta_tokens.astype(x.dtype)], axis=0)[None]
    pos_meta = (jnp.arange(CHUNK) - PAD).astype(F32)
    valid_meta = (pos_meta >= 0).astype(F32)
    ret_s0 = jnp.zeros((RET_HEADS, RET_DK, RET_DV), F32)
    dn_s0 = jnp.zeros((DN_HEADS, DN_DK, DN_DV), F32)
    halo_zero = jnp.zeros((SUBLANES, DN_CONV_CH), BF16)
    h_meta, ret_state = _layer_retention(h_meta, pos_meta, valid_meta, CHUNK, ret_s0, p_ret)
    _, dn_state, proj_meta = _layer_deltanet(h_meta, valid_meta, dn_s0, halo_zero, p_dn, run_tail=False)
    halo_meta = proj_meta[0, CHUNK - SUBLANES:, :DN_CONV_CH]

    pos = (jnp.arange(s) + N_META).astype(F32)
    valid = jnp.ones((s,), F32)
    h, _ = _layer_retention(x, pos, valid, RET_CHUNK, ret_state[0], p_ret)
    out, _, _ = _layer_deltanet(h, valid, dn_state[0], halo_meta, p_dn, final_w=final_norm_w)
    return out
```

```python
import functools

import jax
import jax.numpy as jnp
from jax import lax
from jax.experimental import pallas as pl
from jax.experimental.pallas import tpu as pltpu

F32 = jnp.float32
BF16 = jnp.bfloat16

N_META = 16
CHUNK = 64
PAD = CHUNK - N_META
RMS_EPS = 1e-6
RET_HEADS = 4
RET_DK = 256
RET_DV = 512
RET_QK = RET_HEADS * RET_DK
RET_V = RET_HEADS * RET_DV
ROPE_BASE = 10000.0
DN_HEADS = 8
DN_DK = 128
DN_DV = 256
DN_QK = DN_HEADS * DN_DK
DN_V = DN_HEADS * DN_DV
DN_CONV_CH = 2 * DN_QK + DN_V
CONV_K = 4

LANES = 128
SUBLANES = 8
HALO = 16
VMEM_LIMIT_BYTES = 56 * 1024 * 1024
TOKEN_BLOCK = 512
PROJ_N_CHUNK = 1024
RET_CHUNK = 256
DN_PREP_CHUNKS = 2
DN_REC_BATCH = 2


def _cparams(sem):
    return pltpu.CompilerParams(dimension_semantics=sem, vmem_limit_bytes=VMEM_LIMIT_BYTES)


def _resident(shape):
    nd = len(shape)
    return pl.BlockSpec(shape, lambda *_: (0,) * nd, pipeline_mode=pl.Buffered(1))


def _rms(x, w):
    ms = jnp.mean(x * x, axis=-1, keepdims=True)
    return x * lax.rsqrt(ms + RMS_EPS) * w


def _silu(x):
    return x * jax.nn.sigmoid(x)


def _dot(a, b):
    return jnp.dot(a, b, preferred_element_type=F32)


def _dot_nt(a, b):
    return lax.dot_general(a, b, (((1,), (1,)), ((), ())), preferred_element_type=F32)


def _dot_tn(a, b):
    return lax.dot_general(a, b, (((0,), (0,)), ((), ())), preferred_element_type=F32)


def _rms_inproj_kernel(*refs, has_small):
    if has_small:
        h_ref, nw_ref, w_ref, ws_ref, o_ref, os_ref = refs
    else:
        h_ref, nw_ref, w_ref, o_ref = refs
    xn = _rms(h_ref[...], nw_ref[...]).astype(BF16)
    n_total = o_ref.shape[-1]
    for n0 in range(0, n_total, PROJ_N_CHUNK):
        n1 = min(n0 + PROJ_N_CHUNK, n_total)
        o_ref[:, n0:n1] = _dot(xn, w_ref[:, n0:n1]).astype(o_ref.dtype)
    if has_small:
        os_ref[...] = _dot(xn, ws_ref[...])


def rms_inproj(h2d, norm_w, w, w_small=None):
    m, d = h2d.shape
    n = w.shape[1]
    bm = min(TOKEN_BLOCK, m)
    has_small = w_small is not None
    in_specs = [pl.BlockSpec((bm, d), lambda i: (i, 0)), _resident((1, d)), _resident((d, n))]
    args = [h2d, norm_w.reshape(1, d), w]
    out_shape = [jax.ShapeDtypeStruct((m, n), BF16)]
    out_specs = [pl.BlockSpec((bm, n), lambda i: (i, 0))]
    if has_small:
        ns = w_small.shape[1]
        in_specs.append(_resident((d, ns)))
        args.append(w_small)
        out_shape.append(jax.ShapeDtypeStruct((m, ns), F32))
        out_specs.append(pl.BlockSpec((bm, ns), lambda i: (i, 0)))
    outs = pl.pallas_call(
        functools.partial(_rms_inproj_kernel, has_small=has_small),
        grid=(m // bm,), in_specs=in_specs, out_specs=out_specs, out_shape=out_shape,
        compiler_params=_cparams(("parallel",)), name="rms_inproj")(*args)
    return outs if has_small else outs[0]


def _out_ffn_kernel(*refs, final):
    if final:
        h_ref, o_ref, wo_ref, nw_ref, wg_ref, wu_ref, wd_ref, fw_ref, out_ref = refs
    else:
        h_ref, o_ref, wo_ref, nw_ref, wg_ref, wu_ref, wd_ref, out_ref = refs
    h1 = h_ref[...] + _dot(o_ref[...], wo_ref[...])
    xn = _rms(h1, nw_ref[...]).astype(BF16)
    act = (_silu(_dot(xn, wg_ref[...])) * _dot(xn, wu_ref[...])).astype(BF16)
    h2 = h1 + _dot(act, wd_ref[...])
    if final:
        h2 = _rms(h2, fw_ref[...])
    out_ref[...] = h2


def out_ffn(h2d, o2d, w_out, norm_w, w_gate, w_up, w_down, final_w=None):
    m, d = h2d.shape
    dv = o2d.shape[1]
    f = w_gate.shape[1]
    bm = min(TOKEN_BLOCK, m)
    final = final_w is not None
    row = lambda width: pl.BlockSpec((bm, width), lambda i: (i, 0))
    in_specs = [row(d), row(dv), _resident((dv, d)), _resident((1, d)),
                _resident((d, f)), _resident((d, f)), _resident((f, d))]
    args = [h2d, o2d, w_out, norm_w.reshape(1, d), w_gate, w_up, w_down]
    if final:
        in_specs.append(_resident((1, d)))
        args.append(final_w.reshape(1, d))
    return pl.pallas_call(
        functools.partial(_out_ffn_kernel, final=final),
        grid=(m // bm,), in_specs=in_specs, out_specs=row(d),
        out_shape=jax.ShapeDtypeStruct((m, d), F32),
        compiler_params=_cparams(("parallel",)), name="out_ffn")(*args)


def _retention_kernel(qk_ref, v_ref, g_ref, cos_ref, sin_ref, valid_ref, dmask_ref,
                      xi_ref, zeta_ref, gc_ref, gnw_ref, s0_ref, o_ref, sfin_ref, state):
    c = pl.program_id(1)

    @pl.when(c == 0)
    def _():
        state[...] = s0_ref[...]

    cos = cos_ref[...]
    sin = sin_ref[...]
    valid = valid_ref[...]
    half = RET_DK // 2
    heads = range(RET_HEADS)

    def rope(off):
        t = qk_ref[:, off:off + RET_DK].astype(F32)
        t1, t2 = t[:, :half], t[:, half:]
        return t1 * cos - t2 * sin, t1 * sin + t2 * cos

    q = [rope(h * RET_DK) for h in heads]
    k = [rope(RET_QK + h * RET_DK) for h in heads]
    k = [(k1 * (RET_DK ** -0.5) * valid, k2 * (RET_DK ** -0.5) * valid) for k1, k2 in k]
    cat = lambda a, b: jnp.concatenate([a, b], axis=-1).astype(BF16)
    qb = [cat(q1, q2) for q1, q2 in q]
    kb = [cat(k1, k2) for k1, k2 in k]
    qx = [cat(q1 * xi_ref[h], q2 * xi_ref[h]) for h, (q1, q2) in zip(heads, q)]
    kz = [cat(k1 * zeta_ref[h], k2 * zeta_ref[h]) for h, (k1, k2) in zip(heads, k)]
    vb = [v_ref[:, h * RET_DV:(h + 1) * RET_DV] for h in heads]
    s_prev = [state[h] for h in heads]
    scores = [(_dot_nt(qb[h], kb[h]) * dmask_ref[h]).astype(BF16) for h in heads]
    o = [_dot(scores[h], vb[h]) + _dot(qx[h], s_prev[h].astype(BF16)) for h in heads]
    s_new = [gc_ref[h, 0:1, :] * s_prev[h] + _dot_tn(kz[h], vb[h]) for h in heads]
    state[...] = jnp.stack(s_new, axis=0)

    @pl.when(c == pl.num_programs(1) - 1)
    def _():
        sfin_ref[...] = state[...]

    gnw = gnw_ref[...]
    on = [x * lax.rsqrt(jnp.mean(x * x, axis=-1, keepdims=True) + RMS_EPS) * gnw for x in o]
    o_ref[...] = (jnp.concatenate(on, axis=1) * _silu(g_ref[...].astype(F32))).astype(o_ref.dtype)


def _retention_tables(chunk):
    log_gamma = jnp.log1p(-jnp.exp2(-5.0 - jnp.arange(RET_HEADS, dtype=F32)))
    idx = jnp.arange(chunk, dtype=F32)
    rel = idx[:, None] - idx[None, :]
    dmask = jnp.where((rel >= 0)[None],
                      jnp.exp(log_gamma[:, None, None] * jnp.maximum(rel, 0.0)), 0.0)
    xi = jnp.exp(log_gamma[:, None] * (idx[None, :] + 1.0))
    zeta = jnp.exp(log_gamma[:, None] * (chunk - 1.0 - idx[None, :]))
    gamma_c = jnp.exp(log_gamma * chunk)
    xi = jnp.broadcast_to(xi[:, :, None], (RET_HEADS, chunk, LANES))
    zeta = jnp.broadcast_to(zeta[:, :, None], (RET_HEADS, chunk, LANES))
    gc = jnp.broadcast_to(gamma_c[:, None, None], (RET_HEADS, SUBLANES, RET_DV))
    return dmask, xi, zeta, gc


def _rope_tables(pos):
    half = RET_DK // 2
    inv_freq = ROPE_BASE ** (-jnp.arange(half, dtype=F32) / half)
    ang = pos[:, None] * inv_freq[None, :]
    return jnp.cos(ang), jnp.sin(ang)


def retention(proj, pos, valid, gn_w, s0, chunk):
    b, s, _ = proj.shape
    nc = s // chunk
    cos, sin = _rope_tables(pos)
    dmask, xi, zeta, gc = _retention_tables(chunk)
    valid_tab = jnp.broadcast_to(valid[:, None], (s, LANES))
    assert 2 * RET_QK == RET_V
    tok = lambda blk: pl.BlockSpec((None, chunk, RET_V), lambda bi, c: (bi, c, blk))
    pos_tab = pl.BlockSpec((chunk, LANES), lambda bi, c: (c, 0))
    whole = lambda *shape: pl.BlockSpec(shape, lambda bi, c: (0,) * len(shape))
    in_specs = [tok(0), tok(1), tok(2), pos_tab, pos_tab, pos_tab,
                whole(RET_HEADS, chunk, chunk), whole(RET_HEADS, chunk, LANES),
                whole(RET_HEADS, chunk, LANES), whole(RET_HEADS, SUBLANES, RET_DV),
                whole(1, RET_DV), whole(RET_HEADS, RET_DK, RET_DV)]
    out_specs = [tok(0),
                 pl.BlockSpec((None, RET_HEADS, RET_DK, RET_DV), lambda bi, c: (bi, 0, 0, 0))]
    out_shape = [jax.ShapeDtypeStruct((b, s, RET_V), BF16),
                 jax.ShapeDtypeStruct((b, RET_HEADS, RET_DK, RET_DV), F32)]
    return pl.pallas_call(
        _retention_kernel, grid=(b, nc), in_specs=in_specs, out_specs=out_specs,
        out_shape=out_shape, scratch_shapes=[pltpu.VMEM((RET_HEADS, RET_DK, RET_DV), F32)],
        compiler_params=_cparams(("parallel", "arbitrary")), name="retention")(
            proj, proj, proj, cos, sin, valid_tab, dmask, xi, zeta, gc,
            gn_w.reshape(1, RET_DV), s0)


def _split(x):
    hi = x.astype(BF16)
    lo = (x - hi.astype(F32)).astype(BF16)
    return hi, lo


def _pair_block_diag(y, low):
    zero = jnp.zeros_like(y)
    return jnp.concatenate([jnp.where(low, y, zero), jnp.where(low, zero, y)], axis=0)


def _mm_pair(xs, ys, low):
    (xh, xl), (yh, yl) = xs, ys
    lhs = jnp.concatenate([xh, xh, xl], axis=1)
    rhs = jnp.concatenate([_pair_block_diag(yh, low), _pair_block_diag(yl, low),
                           _pair_block_diag(yh, low)], axis=0)
    return _dot(lhs, rhs)


def _unit_lower_inverses(a_list, row, col, low):
    eye = (row == col).astype(F32)
    blk = lambda n: (row // n) == (col // n)
    mm = functools.partial(_mm_pair, low=low)
    sq = lambda xs: [mm(x, x) for x in xs]
    n1 = [jnp.where(blk(16), -a, 0.0) for a in a_list]
    n2 = sq([_split(x) for x in n1])
    n4 = sq([_split(x) for x in n2])
    n8 = sq([_split(x) for x in n4])
    t = [mm(_split(eye + a), _split(eye + b)) for a, b in zip(n1, n2)]
    t = [mm(_split(x), _split(eye + y)) for x, y in zip(t, n4)]
    t = [mm(_split(x), _split(eye + y)) for x, y in zip(t, n8)]
    for n in (32, 64):
        off_mask = blk(n) & ~blk(n // 2)
        ts = [_split(x) for x in t]
        to = [mm(x, _split(jnp.where(off_mask, a, 0.0))) for x, a in zip(ts, a_list)]
        t = [x - mm(_split(y), xs) for x, y, xs in zip(t, to, ts)]
    return [_split(x) for x in t]


def _dn_prep_kernel(qkv_ref, halo_ref, halo0_ref, small_ref, valid_ref, convw_ref, alog_ref,
                    dtb_ref, u_ref, w_ref, qg_ref, kd_ref, attn_ref, eg_ref, xs, *, cpb):
    c = pl.program_id(1)

    @pl.when(c == 0)
    def _():
        xs[0:HALO, :] = halo0_ref[...]

    @pl.when(c > 0)
    def _():
        xs[0:HALO, :] = halo_ref[...]

    xs[HALO:, :] = qkv_ref[...] * jnp.concatenate([valid_ref[...].astype(BF16)] * (DN_CONV_CH // LANES),
                                                  axis=1)

    sel_r = lax.broadcasted_iota(jnp.int32, ((CONV_K - 1) * CHUNK, HALO + CHUNK), 0)
    sel_c = lax.broadcasted_iota(jnp.int32, ((CONV_K - 1) * CHUNK, HALO + CHUNK), 1)
    shift_sel = (sel_c == (sel_r & (CHUNK - 1)) + (sel_r // CHUNK) + (HALO - (CONV_K - 1))).astype(BF16)

    def conv_silu(j, lo, width):
        r0 = j * CHUNK
        delayed = _dot(shift_sel, xs[r0:r0 + HALO + CHUNK, lo:lo + width])
        acc = (xs[r0 + HALO:r0 + HALO + CHUNK, lo:lo + width].astype(F32)
               * convw_ref[CONV_K - 1:CONV_K, lo:lo + width])
        for i in range(CONV_K - 1):
            acc = acc + delayed[i * CHUNK:(i + 1) * CHUNK] * convw_ref[i:i + 1, lo:lo + width]
        return _silu(acc)

    row = lax.broadcasted_iota(jnp.int32, (CHUNK, CHUNK), 0)
    col = lax.broadcasted_iota(jnp.int32, (CHUNK, CHUNK), 1)
    tri_l = (row >= col).astype(BF16)
    tri_u = (row <= col).astype(BF16)

    def split3(x):
        hi, lo = _split(x)
        lo2 = (x - hi.astype(F32) - lo.astype(F32)).astype(BF16)
        return hi, lo, lo2

    beta_all, gam_all, gam_rows, egam_all, ekd_all = [], [], [], [], []
    for j in range(cpb):
        rs = slice(j * CHUNK, (j + 1) * CHUNK)
        valid = valid_ref[rs, :]
        small = small_ref[rs, :]
        beta_all.append(jax.nn.sigmoid(small) * valid)
        g_all = -jnp.exp(alog_ref[...]) * jax.nn.softplus(small + dtb_ref[...]) * valid
        gam = _dot(jnp.concatenate([tri_l] * 3, axis=1), jnp.concatenate(split3(g_all), axis=0))
        g_rows = g_all.T[DN_HEADS:2 * DN_HEADS, :]
        gam_rows.append(_dot(jnp.concatenate(split3(g_rows), axis=1), jnp.concatenate([tri_u] * 3, axis=0)))
        gam_all.append(gam)
        egam_all.append(jnp.exp(gam))
        ekd_all.append(jnp.exp(gam[CHUNK - 1:CHUNK, :] - gam))

    pairs = range(DN_HEADS // 2)
    units = [(j, p) for j in range(cpb) for p in pairs]
    lane_c = lax.broadcasted_iota(jnp.int32, (CHUNK, 2 * CHUNK), 1)
    prow = lax.broadcasted_iota(jnp.int32, (CHUNK, 2 * CHUNK), 0)
    pcol = lane_c & (CHUNK - 1)
    low_c = lane_c < CHUNK
    low_k = lax.broadcasted_iota(jnp.int32, (CHUNK, 2 * DN_DK), 1) < DN_DK
    p_incl = prow >= pcol
    p_strict = prow > pcol

    def head_cols(x_all, off, p, low):
        a = x_all[:, off + 2 * p:off + 2 * p + 1]
        b = x_all[:, off + 2 * p + 1:off + 2 * p + 2]
        return jnp.where(low, a, b)

    def l2n(x):
        halves = [x[:, :DN_DK], x[:, DN_DK:]]
        return jnp.concatenate(
            [y * lax.rsqrt(jnp.sum(y * y, axis=-1, keepdims=True) + RMS_EPS) for y in halves], axis=1)

    q = [l2n(conv_silu(j, p * 2 * DN_DK, 2 * DN_DK)) * (DN_DK ** -0.5) for j, p in units]
    k = [l2n(conv_silu(j, DN_QK + p * 2 * DN_DK, 2 * DN_DK)) for j, p in units]
    v = [conv_silu(j, 2 * DN_QK + p * 2 * DN_DV, 2 * DN_DV) for j, p in units]
    beta_c = [head_cols(beta_all[j], 0, p, low_c) for j, p in units]
    gcol = [head_cols(gam_all[j], DN_HEADS, p, low_c) for j, p in units]
    grow = [jnp.concatenate([gam_rows[j][2 * p:2 * p + 1, :], gam_rows[j][2 * p + 1:2 * p + 2, :]], axis=1)
            for j, p in units]
    decay = [jnp.exp(jnp.where(p_incl, gc - gr, -jnp.inf)) for gc, gr in zip(gcol, grow)]
    kb = [x.astype(BF16) for x in k]
    zk = jnp.zeros((CHUNK, 2 * DN_DK), BF16)
    kq = [_dot_nt(jnp.concatenate([kbp, qp.astype(BF16)], axis=0),
                  jnp.concatenate([jnp.where(low_k, kbp, zk), jnp.where(low_k, zk, kbp)], axis=0))
          for kbp, qp in zip(kb, q)]
    a_mat = [jnp.where(p_strict, b * x[:CHUNK] * d, 0.0) for b, x, d in zip(beta_c, kq, decay)]
    t = _unit_lower_inverses(a_mat, prow, pcol, low_c)

    zr = jnp.zeros((CHUNK, DN_DV + DN_DK), F32)
    uw = []
    for n, (j, p) in enumerate(units):
        r = []
        for i in range(2):
            h = 2 * p + i
            b = beta_all[j][:, h:h + 1]
            e = egam_all[j][:, DN_HEADS + h:DN_HEADS + h + 1]
            r.append(jnp.concatenate([v[n][:, i * DN_DV:(i + 1) * DN_DV] * b,
                                      k[n][:, i * DN_DK:(i + 1) * DN_DK] * (b * e)], axis=1))
        rhs = jnp.concatenate([jnp.concatenate([r[0], zr], axis=1),
                               jnp.concatenate([zr, r[1]], axis=1)], axis=0).astype(BF16)
        th, tl = t[n]
        uw.append(_dot(jnp.concatenate([th, tl], axis=1), jnp.concatenate([rhs, rhs], axis=0)))

    dvk = DN_DV + DN_DK
    n_pairs = len(pairs)
    rows = lambda per_chunk: jnp.concatenate([per_chunk(j) for j in range(cpb)], axis=0)
    lanes = lambda parts: jnp.concatenate(parts, axis=1)
    u_ref[...] = rows(lambda j: lanes([uw[j * n_pairs + p][:, i * dvk:i * dvk + DN_DV]
                                       for p in pairs for i in range(2)]))
    w_ref[...] = rows(lambda j: lanes([uw[j * n_pairs + p][:, i * dvk + DN_DV:(i + 1) * dvk]
                                       for p in pairs for i in range(2)])).astype(BF16)
    qg_ref[...] = rows(lambda j: lanes([q[j * n_pairs + p] * head_cols(egam_all[j], DN_HEADS, p, low_k)
                                        for p in pairs])).astype(BF16)
    kd_ref[...] = rows(lambda j: lanes([k[j * n_pairs + p] * head_cols(ekd_all[j], DN_HEADS, p, low_k)
                                        for p in pairs])).astype(BF16)
    attn_ref[...] = rows(lambda j: lanes([kq[j * n_pairs + p][CHUNK:] * decay[j * n_pairs + p]
                                          for p in pairs])).astype(BF16)
    eg_ref[...] = rows(lambda j: jnp.broadcast_to(jnp.exp(gam_rows[j][:, CHUNK - 1:CHUNK]),
                                                  (DN_HEADS, DN_DV)))


def dn_prep(proj, small, valid, halo0, conv_w, a_log, dt_bias):
    b, s, _ = proj.shape
    nc = s // CHUNK
    cpb = DN_PREP_CHUNKS if nc % DN_PREP_CHUNKS == 0 else 1
    rows = cpb * CHUNK
    hb = rows // HALO
    valid_tab = jnp.broadcast_to(valid[:, None], (s, LANES))
    pad_heads = lambda v: jnp.zeros((1, LANES), F32).at[0, DN_HEADS:2 * DN_HEADS].set(v)
    tok = lambda width: pl.BlockSpec((None, rows, width), lambda bi, c: (bi, c, 0))
    in_specs = [tok(DN_CONV_CH),
                pl.BlockSpec((None, HALO, DN_CONV_CH), lambda bi, c: (bi, jnp.maximum(c * hb - 1, 0), 0)),
                pl.BlockSpec((HALO, DN_CONV_CH), lambda bi, c: (0, 0)),
                tok(LANES),
                pl.BlockSpec((rows, LANES), lambda bi, c: (c, 0)),
                pl.BlockSpec((CONV_K, DN_CONV_CH), lambda bi, c: (0, 0)),
                pl.BlockSpec((1, LANES), lambda bi, c: (0, 0)),
                pl.BlockSpec((1, LANES), lambda bi, c: (0, 0))]
    out_specs = [tok(DN_V), tok(DN_QK), tok(DN_QK), tok(DN_QK), tok(DN_HEADS * CHUNK),
                 pl.BlockSpec((None, cpb * DN_HEADS, DN_DV), lambda bi, c: (bi, c, 0))]
    out_shape = [jax.ShapeDtypeStruct((b, s, DN_V), F32),
                 jax.ShapeDtypeStruct((b, s, DN_QK), BF16),
                 jax.ShapeDtypeStruct((b, s, DN_QK), BF16),
                 jax.ShapeDtypeStruct((b, s, DN_QK), BF16),
                 jax.ShapeDtypeStruct((b, s, DN_HEADS * CHUNK), BF16),
                 jax.ShapeDtypeStruct((b, nc * DN_HEADS, DN_DV), F32)]
    return pl.pallas_call(
        functools.partial(_dn_prep_kernel, cpb=cpb),
        grid=(b, nc // cpb), in_specs=in_specs, out_specs=out_specs, out_shape=out_shape,
        scratch_shapes=[pltpu.VMEM((HALO + rows, DN_CONV_CH), BF16)],
        compiler_params=_cparams(("parallel", "arbitrary")), name="dn_prep")(
            proj, proj, halo0, small, valid_tab, conv_w, pad_heads(a_log), pad_heads(dt_bias))


def _dn_rec_kernel(u_ref, w_ref, qg_ref, kd_ref, attn_ref, eg_ref, gate_ref, nw_ref, s0_ref,
                   o_ref, sfin_ref, state, *, bb):
    c = pl.program_id(1)

    @pl.when(c == 0)
    def _():
        for r in range(bb):
            state[r] = s0_ref[...]

    units = [(r, h) for r in range(bb) for h in range(DN_HEADS)]
    dk = lambda h: slice(h * DN_DK, (h + 1) * DN_DK)
    dv = lambda h: slice(h * DN_DV, (h + 1) * DN_DV)
    s_prev = [state[r, h] for r, h in units]
    wq = [jnp.concatenate([w_ref[r, :, dk(h)], qg_ref[r, :, dk(h)]], axis=0) for r, h in units]
    ws = [_dot(x, s.astype(BF16)) for x, s in zip(wq, s_prev)]
    vb = [(u_ref[r, :, dv(h)] - x[:CHUNK]).astype(BF16) for (r, h), x in zip(units, ws)]
    o = [x[CHUNK:] + _dot(attn_ref[r, :, h * CHUNK:(h + 1) * CHUNK], y)
         for (r, h), x, y in zip(units, ws, vb)]
    s_new = [s * eg_ref[r, h:h + 1, :] + _dot_tn(kd_ref[r, :, dk(h)], y)
             for (r, h), s, y in zip(units, s_prev, vb)]
    state[...] = jnp.stack(s_new, axis=0).reshape(state.shape)
    nw = nw_ref[...]
    on = [x * lax.rsqrt(jnp.mean(x * x, axis=-1, keepdims=True) + RMS_EPS) * nw for x in o]
    for r in range(bb):
        o_r = jnp.concatenate(on[r * DN_HEADS:(r + 1) * DN_HEADS], axis=1)
        o_ref[r] = (o_r * _silu(gate_ref[r].astype(F32))).astype(o_ref.dtype)

    @pl.when(c == pl.num_programs(1) - 1)
    def _():
        sfin_ref[...] = state[...]


def dn_rec(u, w, qg, kd, attn, eg, proj, norm_w, s0):
    b, s, _ = u.shape
    nc = s // CHUNK
    bb = DN_REC_BATCH if b % DN_REC_BATCH == 0 else 1
    gate_blk = DN_CONV_CH // DN_V
    tok = lambda width, off=0: pl.BlockSpec((bb, CHUNK, width), lambda bi, c: (bi, c, off))
    in_specs = [tok(DN_V), tok(DN_QK), tok(DN_QK), tok(DN_QK), tok(DN_HEADS * CHUNK),
                pl.BlockSpec((bb, DN_HEADS, DN_DV), lambda bi, c: (bi, c, 0)),
                tok(DN_V, gate_blk),
                pl.BlockSpec((1, DN_DV), lambda bi, c: (0, 0)),
                pl.BlockSpec((DN_HEADS, DN_DK, DN_DV), lambda bi, c: (0, 0, 0))]
    out_specs = [tok(DN_V),
                 pl.BlockSpec((bb, DN_HEADS, DN_DK, DN_DV), lambda bi, c: (bi, 0, 0, 0))]
    out_shape = [jax.ShapeDtypeStruct((b, s, DN_V), BF16),
                 jax.ShapeDtypeStruct((b, DN_HEADS, DN_DK, DN_DV), F32)]
    return pl.pallas_call(
        functools.partial(_dn_rec_kernel, bb=bb),
        grid=(b // bb, nc), in_specs=in_specs, out_specs=out_specs, out_shape=out_shape,
        scratch_shapes=[pltpu.VMEM((bb, DN_HEADS, DN_DK, DN_DV), F32)],
        compiler_params=_cparams(("parallel", "arbitrary")), name="dn_rec")(
            u, w, qg, kd, attn, eg, proj, norm_w.reshape(1, DN_DV), s0)


def _layer_retention(h, pos, valid, chunk, s0, p):
    b, s, d = h.shape
    h2 = h.reshape(b * s, d)
    proj = rms_inproj(h2, p["mix_norm_w"], p["w_in"]).reshape(b, s, -1)
    o, s_fin = retention(proj, pos, valid, p["gn_w"], s0, chunk)
    h2 = out_ffn(h2, o.reshape(b * s, RET_V), p["w_out"], p["ffn_norm_w"], p["w_gate"], p["w_up"],
                 p["w_down"])
    return h2.reshape(b, s, d), s_fin


def _layer_deltanet(h, valid, s0, halo0, p, final_w=None, run_tail=True):
    b, s, d = h.shape
    h2 = h.reshape(b * s, d)
    proj, small = rms_inproj(h2, p["mix_norm_w"], p["w_in"], p["w_in_small"])
    proj = proj.reshape(b, s, -1)
    small = small.reshape(b, s, -1)
    u, w, qg, kd, attn, eg = dn_prep(proj, small, valid, halo0, p["conv_w"], p["a_log"], p["dt_bias"])
    o, s_fin = dn_rec(u, w, qg, kd, attn, eg, proj, p["norm_w"], s0)
    out = None
    if run_tail:
        out = out_ffn(h2, o.reshape(b * s, DN_V), p["w_out"], p["ffn_norm_w"], p["w_gate"], p["w_up"],
                      p["w_down"], final_w).reshape(b, s, d)
    return out, s_fin, proj


def kernel(x, meta_tokens, mix_norm_w, ffn_norm_w, ret_w_in, ret_gn_w, ret_w_out, dn_w_in, dn_conv_w,
           dn_a_log, dn_dt_bias, dn_norm_w, dn_w_out, ffn_w_gate, ffn_w_up, ffn_w_down, final_norm_w):
    b, s, d = x.shape
    bf = lambda t: t.astype(BF16)
    dn_main = DN_CONV_CH + DN_V
    small_w = jnp.zeros((d, LANES), F32).at[:, :2 * DN_HEADS].set(dn_w_in[0][:, dn_main:])
    p_ret = dict(mix_norm_w=mix_norm_w[0], w_in=bf(ret_w_in[0]), gn_w=ret_gn_w[0], w_out=bf(ret_w_out[0]),
                 ffn_norm_w=ffn_norm_w[0], w_gate=bf(ffn_w_gate[0]), w_up=bf(ffn_w_up[0]),
                 w_down=bf(ffn_w_down[0]))
    p_dn = dict(mix_norm_w=mix_norm_w[1], w_in=bf(dn_w_in[0][:, :dn_main]), w_in_small=bf(small_w),
                conv_w=dn_conv_w[0], a_log=dn_a_log[0], dt_bias=dn_dt_bias[0], norm_w=dn_norm_w[0],
                w_out=bf(dn_w_out[0]), ffn_norm_w=ffn_norm_w[1], w_gate=bf(ffn_w_gate[1]),
                w_up=bf(ffn_w_up[1]), w_down=bf(ffn_w_down[1]))

    h_meta = jnp.concatenate([jnp.zeros((PAD, d), x.dtype), meta_tokens.astype(x.dtype)], axis=0)[None]
    pos_meta = (jnp.arange(CHUNK) - PAD).astype(F32)
    valid_meta = (pos_meta >= 0).astype(F32)
    ret_s0 = jnp.zeros((RET_HEADS, RET_DK, RET_DV), F32)
    dn_s0 = jnp.zeros((DN_HEADS, DN_DK, DN_DV), F32)
    halo_zero = jnp.zeros((HALO, DN_CONV_CH), BF16)
    h_meta, ret_state = _layer_retention(h_meta, pos_meta, valid_meta, CHUNK, ret_s0, p_ret)
    _, dn_state, proj_meta = _layer_deltanet(h_meta, valid_meta, dn_s0, halo_zero, p_dn, run_tail=False)
    halo_meta = proj_meta[0, CHUNK - HALO:, :DN_CONV_CH]

    pos = (jnp.arange(s) + N_META).astype(F32)
    valid = jnp.ones((s,), F32)
    h, _ = _layer_retention(x, pos, valid, RET_CHUNK, ret_state[0], p_ret)
    out, _, _ = _layer_deltanet(h, valid, dn_state[0], halo_meta, p_dn, final_w=final_norm_w)
    return out
```

```python
import functools

import jax
import jax.numpy as jnp
from jax import lax
from jax.experimental import pallas as pl
from jax.experimental.pallas import tpu as pltpu

F32 = jnp.float32
BF16 = jnp.bfloat16

N_META = 16
CHUNK = 64
PAD = CHUNK - N_META
RMS_EPS = 1e-6
RET_HEADS = 4
RET_DK = 256
RET_DV = 512
RET_QK = RET_HEADS * RET_DK
RET_V = RET_HEADS * RET_DV
ROPE_BASE = 10000.0
DN_HEADS = 8
DN_DK = 128
DN_DV = 256
DN_QK = DN_HEADS * DN_DK
DN_V = DN_HEADS * DN_DV
DN_CONV_CH = 2 * DN_QK + DN_V
CONV_K = 4

LANES = 128
SUBLANES = 8
HALO = 16
VMEM_LIMIT_BYTES = 56 * 1024 * 1024
TOKEN_BLOCK = 512
PROJ_N_CHUNK = 1024
RET_CHUNK = 256
DN_CHUNKS_PER_STEP = 4


def _cparams(sem):
    return pltpu.CompilerParams(dimension_semantics=sem, vmem_limit_bytes=VMEM_LIMIT_BYTES)


def _resident(shape):
    nd = len(shape)
    return pl.BlockSpec(shape, lambda *_: (0,) * nd, pipeline_mode=pl.Buffered(1))


def _rms(x, w):
    ms = jnp.mean(x * x, axis=-1, keepdims=True)
    return x * lax.rsqrt(ms + RMS_EPS) * w


def _silu(x):
    return x * jax.nn.sigmoid(x)


def _dot(a, b):
    return jnp.dot(a, b, preferred_element_type=F32)


def _dot_nt(a, b):
    return lax.dot_general(a, b, (((1,), (1,)), ((), ())), preferred_element_type=F32)


def _dot_tn(a, b):
    return lax.dot_general(a, b, (((0,), (0,)), ((), ())), preferred_element_type=F32)


def _rms_inproj_kernel(*refs, has_small):
    if has_small:
        h_ref, nw_ref, w_ref, ws_ref, o_ref, os_ref = refs
    else:
        h_ref, nw_ref, w_ref, o_ref = refs
    xn = _rms(h_ref[...], nw_ref[...]).astype(BF16)
    n_total = o_ref.shape[-1]
    for n0 in range(0, n_total, PROJ_N_CHUNK):
        n1 = min(n0 + PROJ_N_CHUNK, n_total)
        o_ref[:, n0:n1] = _dot(xn, w_ref[:, n0:n1]).astype(o_ref.dtype)
    if has_small:
        os_ref[...] = _dot(xn, ws_ref[...])


def rms_inproj(h2d, norm_w, w, w_small=None):
    m, d = h2d.shape
    n = w.shape[1]
    bm = min(TOKEN_BLOCK, m)
    has_small = w_small is not None
    in_specs = [pl.BlockSpec((bm, d), lambda i: (i, 0)), _resident((1, d)), _resident((d, n))]
    args = [h2d, norm_w.reshape(1, d), w]
    out_shape = [jax.ShapeDtypeStruct((m, n), BF16)]
    out_specs = [pl.BlockSpec((bm, n), lambda i: (i, 0))]
    if has_small:
        ns = w_small.shape[1]
        in_specs.append(_resident((d, ns)))
        args.append(w_small)
        out_shape.append(jax.ShapeDtypeStruct((m, ns), F32))
        out_specs.append(pl.BlockSpec((bm, ns), lambda i: (i, 0)))
    outs = pl.pallas_call(
        functools.partial(_rms_inproj_kernel, has_small=has_small),
        grid=(m // bm,), in_specs=in_specs, out_specs=out_specs, out_shape=out_shape,
        compiler_params=_cparams(("parallel",)), name="rms_inproj")(*args)
    return outs if has_small else outs[0]


def _out_ffn_kernel(*refs, final):
    if final:
        h_ref, o_ref, wo_ref, nw_ref, wg_ref, wu_ref, wd_ref, fw_ref, out_ref = refs
    else:
        h_ref, o_ref, wo_ref, nw_ref, wg_ref, wu_ref, wd_ref, out_ref = refs
    h1 = h_ref[...] + _dot(o_ref[...], wo_ref[...])
    xn = _rms(h1, nw_ref[...]).astype(BF16)
    act = (_silu(_dot(xn, wg_ref[...])) * _dot(xn, wu_ref[...])).astype(BF16)
    h2 = h1 + _dot(act, wd_ref[...])
    if final:
        h2 = _rms(h2, fw_ref[...])
    out_ref[...] = h2


def out_ffn(h2d, o2d, w_out, norm_w, w_gate, w_up, w_down, final_w=None):
    m, d = h2d.shape
    dv = o2d.shape[1]
    f = w_gate.shape[1]
    bm = min(TOKEN_BLOCK, m)
    final = final_w is not None
    row = lambda width: pl.BlockSpec((bm, width), lambda i: (i, 0))
    in_specs = [row(d), row(dv), _resident((dv, d)), _resident((1, d)),
                _resident((d, f)), _resident((d, f)), _resident((f, d))]
    args = [h2d, o2d, w_out, norm_w.reshape(1, d), w_gate, w_up, w_down]
    if final:
        in_specs.append(_resident((1, d)))
        args.append(final_w.reshape(1, d))
    return pl.pallas_call(
        functools.partial(_out_ffn_kernel, final=final),
        grid=(m // bm,), in_specs=in_specs, out_specs=row(d),
        out_shape=jax.ShapeDtypeStruct((m, d), F32),
        compiler_params=_cparams(("parallel",)), name="out_ffn")(*args)


def _retention_kernel(qk_ref, v_ref, g_ref, cos_ref, sin_ref, valid_ref, dmask_ref,
                      xi_ref, zeta_ref, gc_ref, gnw_ref, s0_ref, o_ref, sfin_ref, state):
    c = pl.program_id(1)

    @pl.when(c == 0)
    def _():
        state[...] = s0_ref[...]

    cos = cos_ref[...]
    sin = sin_ref[...]
    valid = valid_ref[...]
    half = RET_DK // 2
    heads = range(RET_HEADS)

    def rope(off):
        t = qk_ref[:, off:off + RET_DK].astype(F32)
        t1, t2 = t[:, :half], t[:, half:]
        return t1 * cos - t2 * sin, t1 * sin + t2 * cos

    q = [rope(h * RET_DK) for h in heads]
    k = [rope(RET_QK + h * RET_DK) for h in heads]
    k = [(k1 * (RET_DK ** -0.5) * valid, k2 * (RET_DK ** -0.5) * valid) for k1, k2 in k]
    cat = lambda a, b: jnp.concatenate([a, b], axis=-1).astype(BF16)
    qb = [cat(q1, q2) for q1, q2 in q]
    kb = [cat(k1, k2) for k1, k2 in k]
    qx = [cat(q1 * xi_ref[h], q2 * xi_ref[h]) for h, (q1, q2) in zip(heads, q)]
    kz = [cat(k1 * zeta_ref[h], k2 * zeta_ref[h]) for h, (k1, k2) in zip(heads, k)]
    vb = [v_ref[:, h * RET_DV:(h + 1) * RET_DV] for h in heads]
    s_prev = [state[h] for h in heads]
    scores = [(_dot_nt(qb[h], kb[h]) * dmask_ref[h]).astype(BF16) for h in heads]
    o = [_dot(scores[h], vb[h]) + _dot(qx[h], s_prev[h].astype(BF16)) for h in heads]
    s_new = [gc_ref[h, 0:1, :] * s_prev[h] + _dot_tn(kz[h], vb[h]) for h in heads]
    state[...] = jnp.stack(s_new, axis=0)

    @pl.when(c == pl.num_programs(1) - 1)
    def _():
        sfin_ref[...] = state[...]

    gnw = gnw_ref[...]
    on = [x * lax.rsqrt(jnp.mean(x * x, axis=-1, keepdims=True) + RMS_EPS) * gnw for x in o]
    o_ref[...] = (jnp.concatenate(on, axis=1) * _silu(g_ref[...].astype(F32))).astype(o_ref.dtype)


def _retention_tables(chunk):
    log_gamma = jnp.log1p(-jnp.exp2(-5.0 - jnp.arange(RET_HEADS, dtype=F32)))
    idx = jnp.arange(chunk, dtype=F32)
    rel = idx[:, None] - idx[None, :]
    dmask = jnp.where((rel >= 0)[None],
                      jnp.exp(log_gamma[:, None, None] * jnp.maximum(rel, 0.0)), 0.0)
    xi = jnp.exp(log_gamma[:, None] * (idx[None, :] + 1.0))
    zeta = jnp.exp(log_gamma[:, None] * (chunk - 1.0 - idx[None, :]))
    gamma_c = jnp.exp(log_gamma * chunk)
    xi = jnp.broadcast_to(xi[:, :, None], (RET_HEADS, chunk, LANES))
    zeta = jnp.broadcast_to(zeta[:, :, None], (RET_HEADS, chunk, LANES))
    gc = jnp.broadcast_to(gamma_c[:, None, None], (RET_HEADS, SUBLANES, RET_DV))
    return dmask, xi, zeta, gc


def _rope_tables(pos):
    half = RET_DK // 2
    inv_freq = ROPE_BASE ** (-jnp.arange(half, dtype=F32) / half)
    ang = pos[:, None] * inv_freq[None, :]
    return jnp.cos(ang), jnp.sin(ang)


def retention(proj, pos, valid, gn_w, s0, chunk):
    b, s, _ = proj.shape
    nc = s // chunk
    cos, sin = _rope_tables(pos)
    dmask, xi, zeta, gc = _retention_tables(chunk)
    valid_tab = jnp.broadcast_to(valid[:, None], (s, LANES))
    assert 2 * RET_QK == RET_V
    tok = lambda blk: pl.BlockSpec((None, chunk, RET_V), lambda bi, c: (bi, c, blk))
    pos_tab = pl.BlockSpec((chunk, LANES), lambda bi, c: (c, 0))
    whole = lambda *shape: pl.BlockSpec(shape, lambda bi, c: (0,) * len(shape))
    in_specs = [tok(0), tok(1), tok(2), pos_tab, pos_tab, pos_tab,
                whole(RET_HEADS, chunk, chunk), whole(RET_HEADS, chunk, LANES),
                whole(RET_HEADS, chunk, LANES), whole(RET_HEADS, SUBLANES, RET_DV),
                whole(1, RET_DV), whole(RET_HEADS, RET_DK, RET_DV)]
    out_specs = [tok(0),
                 pl.BlockSpec((None, RET_HEADS, RET_DK, RET_DV), lambda bi, c: (bi, 0, 0, 0))]
    out_shape = [jax.ShapeDtypeStruct((b, s, RET_V), BF16),
                 jax.ShapeDtypeStruct((b, RET_HEADS, RET_DK, RET_DV), F32)]
    return pl.pallas_call(
        _retention_kernel, grid=(b, nc), in_specs=in_specs, out_specs=out_specs,
        out_shape=out_shape, scratch_shapes=[pltpu.VMEM((RET_HEADS, RET_DK, RET_DV), F32)],
        compiler_params=_cparams(("parallel", "arbitrary")), name="retention")(
            proj, proj, proj, cos, sin, valid_tab, dmask, xi, zeta, gc,
            gn_w.reshape(1, RET_DV), s0)


def _split(x):
    hi = x.astype(BF16)
    lo = (x - hi.astype(F32)).astype(BF16)
    return hi, lo


def _pair_block_diag(y, low):
    zero = jnp.zeros_like(y)
    return jnp.concatenate([jnp.where(low, y, zero), jnp.where(low, zero, y)], axis=0)


def _mm_pair(xs, ys, low):
    (xh, xl), (yh, yl) = xs, ys
    lhs = jnp.concatenate([xh, xh, xl], axis=1)
    rhs = jnp.concatenate([_pair_block_diag(yh, low), _pair_block_diag(yl, low),
                           _pair_block_diag(yh, low)], axis=0)
    return _dot(lhs, rhs)


def _unit_lower_inverses(a_list, row, col, low):
    eye = (row == col).astype(F32)
    blk = lambda n: (row // n) == (col // n)
    mm = functools.partial(_mm_pair, low=low)
    sq = lambda xs: [mm(x, x) for x in xs]
    n1 = [jnp.where(blk(16), -a, 0.0) for a in a_list]
    n2 = sq([_split(x) for x in n1])
    n4 = sq([_split(x) for x in n2])
    n8 = sq([_split(x) for x in n4])
    t = [mm(_split(eye + a), _split(eye + b)) for a, b in zip(n1, n2)]
    t = [mm(_split(x), _split(eye + y)) for x, y in zip(t, n4)]
    t = [mm(_split(x), _split(eye + y)) for x, y in zip(t, n8)]
    for n in (32, 64):
        off_mask = blk(n) & ~blk(n // 2)
        ts = [_split(x) for x in t]
        to = [mm(x, _split(jnp.where(off_mask, a, 0.0))) for x, a in zip(ts, a_list)]
        t = [x - mm(_split(y), xs) for x, y, xs in zip(t, to, ts)]
    return [_split(x) for x in t]


def _dn_recurrence(u_s, w_s, qg_s, kd_s, attn_s, eg_s, gate_ref, nw_ref, state, cpb):
    heads = range(DN_HEADS)
    dk = [slice(h * DN_DK, (h + 1) * DN_DK) for h in heads]
    dv = [slice(h * DN_DV, (h + 1) * DN_DV) for h in heads]
    nw = nw_ref[...]
    s_cur = [state[h] for h in heads]
    out_rows = []
    for j in range(cpb):
        rs = slice(j * CHUNK, (j + 1) * CHUNK)
        wq = [jnp.concatenate([w_s[rs, dk[h]], qg_s[rs, dk[h]]], axis=0) for h in heads]
        ws = [_dot(wq[h], s_cur[h].astype(BF16)) for h in heads]
        vb = [(u_s[rs, dv[h]] - ws[h][:CHUNK]).astype(BF16) for h in heads]
        o = [ws[h][CHUNK:] + _dot(attn_s[rs, h * CHUNK:(h + 1) * CHUNK], vb[h]) for h in heads]
        s_cur = [s_cur[h] * eg_s[j * DN_HEADS + h:j * DN_HEADS + h + 1, :] + _dot_tn(kd_s[rs, dk[h]], vb[h])
                 for h in heads]
        on = [x * lax.rsqrt(jnp.mean(x * x, axis=-1, keepdims=True) + RMS_EPS) * nw for x in o]
        out_rows.append(jnp.concatenate(on, axis=1) * _silu(gate_ref[rs, :].astype(F32)))
    state[...] = jnp.stack(s_cur, axis=0)
    return jnp.concatenate(out_rows, axis=0)


def _deltanet_kernel(qkv_ref, halo_ref, halo0_ref, small_ref, valid_ref, convw_ref, alog_ref, dtb_ref,
                     gate_ref, nw_ref, s0_ref, o_ref, sfin_ref,
                     xs, u_ref, w_ref, qg_ref, kd_ref, attn_ref, eg_ref, state, *, cpb):
    c = pl.program_id(1)
    last_work = pl.num_programs(1) - 2

    @pl.when(c == 0)
    def _():
        for ref in (u_ref, w_ref, qg_ref, kd_ref, attn_ref, eg_ref):
            ref[...] = jnp.zeros(ref.shape, ref.dtype)

    @pl.when(c <= 1)
    def _():
        state[...] = s0_ref[...]

    @pl.when(jnp.minimum(c, last_work) == 0)
    def _():
        xs[0:HALO, :] = halo0_ref[...]

    @pl.when(jnp.minimum(c, last_work) > 0)
    def _():
        xs[0:HALO, :] = halo_ref[...]

    o_ref[...] = _dn_recurrence(u_ref, w_ref, qg_ref, kd_ref, attn_ref, eg_ref, gate_ref, nw_ref,
                                state, cpb).astype(o_ref.dtype)

    xs[HALO:, :] = qkv_ref[...] * jnp.concatenate([valid_ref[...].astype(BF16)] * (DN_CONV_CH // LANES),
                                                  axis=1)

    sel_r = lax.broadcasted_iota(jnp.int32, ((CONV_K - 1) * CHUNK, HALO + CHUNK), 0)
    sel_c = lax.broadcasted_iota(jnp.int32, ((CONV_K - 1) * CHUNK, HALO + CHUNK), 1)
    shift_sel = (sel_c == (sel_r & (CHUNK - 1)) + (sel_r // CHUNK) + (HALO - (CONV_K - 1))).astype(BF16)

    def conv_silu(j, lo, width):
        r0 = j * CHUNK
        delayed = _dot(shift_sel, xs[r0:r0 + HALO + CHUNK, lo:lo + width])
        acc = (xs[r0 + HALO:r0 + HALO + CHUNK, lo:lo + width].astype(F32)
               * convw_ref[CONV_K - 1:CONV_K, lo:lo + width])
        for i in range(CONV_K - 1):
            acc = acc + delayed[i * CHUNK:(i + 1) * CHUNK] * convw_ref[i:i + 1, lo:lo + width]
        return _silu(acc)

    row = lax.broadcasted_iota(jnp.int32, (CHUNK, CHUNK), 0)
    col = lax.broadcasted_iota(jnp.int32, (CHUNK, CHUNK), 1)
    tri_l = (row >= col).astype(BF16)
    tri_u = (row <= col).astype(BF16)

    def split3(x):
        hi, lo = _split(x)
        lo2 = (x - hi.astype(F32) - lo.astype(F32)).astype(BF16)
        return hi, lo, lo2

    beta_all, gam_all, gam_rows, egam_all, ekd_all = [], [], [], [], []
    for j in range(cpb):
        rs = slice(j * CHUNK, (j + 1) * CHUNK)
        valid = valid_ref[rs, :]
        small = small_ref[rs, :]
        beta_all.append(jax.nn.sigmoid(small) * valid)
        g_all = -jnp.exp(alog_ref[...]) * jax.nn.softplus(small + dtb_ref[...]) * valid
        gam = _dot(jnp.concatenate([tri_l] * 3, axis=1), jnp.concatenate(split3(g_all), axis=0))
        g_rows = g_all.T[DN_HEADS:2 * DN_HEADS, :]
        gam_rows.append(_dot(jnp.concatenate(split3(g_rows), axis=1), jnp.concatenate([tri_u] * 3, axis=0)))
        gam_all.append(gam)
        egam_all.append(jnp.exp(gam))
        ekd_all.append(jnp.exp(gam[CHUNK - 1:CHUNK, :] - gam))

    pairs = range(DN_HEADS // 2)
    units = [(j, p) for j in range(cpb) for p in pairs]
    lane_c = lax.broadcasted_iota(jnp.int32, (CHUNK, 2 * CHUNK), 1)
    prow = lax.broadcasted_iota(jnp.int32, (CHUNK, 2 * CHUNK), 0)
    pcol = lane_c & (CHUNK - 1)
    low_c = lane_c < CHUNK
    low_k = lax.broadcasted_iota(jnp.int32, (CHUNK, 2 * DN_DK), 1) < DN_DK
    p_incl = prow >= pcol
    p_strict = prow > pcol

    def head_cols(x_all, off, p, low):
        a = x_all[:, off + 2 * p:off + 2 * p + 1]
        b = x_all[:, off + 2 * p + 1:off + 2 * p + 2]
        return jnp.where(low, a, b)

    def l2n(x):
        halves = [x[:, :DN_DK], x[:, DN_DK:]]
        return jnp.concatenate(
            [y * lax.rsqrt(jnp.sum(y * y, axis=-1, keepdims=True) + RMS_EPS) for y in halves], axis=1)

    q = [l2n(conv_silu(j, p * 2 * DN_DK, 2 * DN_DK)) * (DN_DK ** -0.5) for j, p in units]
    k = [l2n(conv_silu(j, DN_QK + p * 2 * DN_DK, 2 * DN_DK)) for j, p in units]
    v = [conv_silu(j, 2 * DN_QK + p * 2 * DN_DV, 2 * DN_DV) for j, p in units]
    beta_c = [head_cols(beta_all[j], 0, p, low_c) for j, p in units]
    gcol = [head_cols(gam_all[j], DN_HEADS, p, low_c) for j, p in units]
    grow = [jnp.concatenate([gam_rows[j][2 * p:2 * p + 1, :], gam_rows[j][2 * p + 1:2 * p + 2, :]], axis=1)
            for j, p in units]
    decay = [jnp.exp(jnp.where(p_incl, gc - gr, -jnp.inf)) for gc, gr in zip(gcol, grow)]
    kb = [x.astype(BF16) for x in k]
    zk = jnp.zeros((CHUNK, 2 * DN_DK), BF16)
    kq = [_dot_nt(jnp.concatenate([kbp, qp.astype(BF16)], axis=0),
                  jnp.concatenate([jnp.where(low_k, kbp, zk), jnp.where(low_k, zk, kbp)], axis=0))
          for kbp, qp in zip(kb, q)]
    a_mat = [jnp.where(p_strict, b * x[:CHUNK] * d, 0.0) for b, x, d in zip(beta_c, kq, decay)]
    t = _unit_lower_inverses(a_mat, prow, pcol, low_c)

    zr = jnp.zeros((CHUNK, DN_DV + DN_DK), F32)
    uw = []
    for n, (j, p) in enumerate(units):
        r = []
        for i in range(2):
            h = 2 * p + i
            b = beta_all[j][:, h:h + 1]
            e = egam_all[j][:, DN_HEADS + h:DN_HEADS + h + 1]
            r.append(jnp.concatenate([v[n][:, i * DN_DV:(i + 1) * DN_DV] * b,
                                      k[n][:, i * DN_DK:(i + 1) * DN_DK] * (b * e)], axis=1))
        rhs = jnp.concatenate([jnp.concatenate([r[0], zr], axis=1),
                               jnp.concatenate([zr, r[1]], axis=1)], axis=0).astype(BF16)
        th, tl = t[n]
        uw.append(_dot(jnp.concatenate([th, tl], axis=1), jnp.concatenate([rhs, rhs], axis=0)))

    dvk = DN_DV + DN_DK
    n_pairs = len(pairs)
    rows = lambda per_chunk: jnp.concatenate([per_chunk(j) for j in range(cpb)], axis=0)
    lanes = lambda parts: jnp.concatenate(parts, axis=1)
    u_ref[...] = rows(lambda j: lanes([uw[j * n_pairs + p][:, i * dvk:i * dvk + DN_DV]
                                       for p in pairs for i in range(2)]))
    w_ref[...] = rows(lambda j: lanes([uw[j * n_pairs + p][:, i * dvk + DN_DV:(i + 1) * dvk]
                                       for p in pairs for i in range(2)])).astype(BF16)
    qg_ref[...] = rows(lambda j: lanes([q[j * n_pairs + p] * head_cols(egam_all[j], DN_HEADS, p, low_k)
                                        for p in pairs])).astype(BF16)
    kd_ref[...] = rows(lambda j: lanes([k[j * n_pairs + p] * head_cols(ekd_all[j], DN_HEADS, p, low_k)
                                        for p in pairs])).astype(BF16)
    attn_ref[...] = rows(lambda j: lanes([kq[j * n_pairs + p][CHUNK:] * decay[j * n_pairs + p]
                                          for p in pairs])).astype(BF16)
    eg_ref[...] = rows(lambda j: jnp.broadcast_to(jnp.exp(gam_rows[j][:, CHUNK - 1:CHUNK]),
                                                  (DN_HEADS, DN_DV)))

    @pl.when(c == pl.num_programs(1) - 1)
    def _():
        sfin_ref[...] = state[...]


def deltanet_mixer(proj, small, valid, halo0, conv_w, a_log, dt_bias, norm_w, s0):
    b, s, _ = proj.shape
    nc = s // CHUNK
    cpb = DN_CHUNKS_PER_STEP if nc % DN_CHUNKS_PER_STEP == 0 else 1
    rows = cpb * CHUNK
    steps = nc // cpb
    hb = rows // HALO
    gate_blk = DN_CONV_CH // DN_V
    valid_tab = jnp.broadcast_to(valid[:, None], (s, LANES))
    pad_heads = lambda v: jnp.zeros((1, LANES), F32).at[0, DN_HEADS:2 * DN_HEADS].set(v)
    cur = lambda c: jnp.minimum(c, steps - 1)
    prev = lambda c: jnp.maximum(c - 1, 0)
    const = lambda *shape: pl.BlockSpec(shape, lambda bi, c: (0,) * len(shape))
    in_specs = [pl.BlockSpec((None, rows, DN_CONV_CH), lambda bi, c: (bi, cur(c), 0)),
                pl.BlockSpec((None, HALO, DN_CONV_CH), lambda bi, c: (bi, jnp.maximum(cur(c) * hb - 1, 0), 0)),
                const(HALO, DN_CONV_CH),
                pl.BlockSpec((None, rows, LANES), lambda bi, c: (bi, cur(c), 0)),
                pl.BlockSpec((rows, LANES), lambda bi, c: (cur(c), 0)),
                const(CONV_K, DN_CONV_CH), const(1, LANES), const(1, LANES),
                pl.BlockSpec((None, rows, DN_V), lambda bi, c: (bi, prev(c), gate_blk)),
                const(1, DN_DV), const(DN_HEADS, DN_DK, DN_DV)]
    out_specs = [pl.BlockSpec((None, rows, DN_V), lambda bi, c: (bi, prev(c), 0)),
                 pl.BlockSpec((None, DN_HEADS, DN_DK, DN_DV), lambda bi, c: (bi, 0, 0, 0))]
    out_shape = [jax.ShapeDtypeStruct((b, s, DN_V), BF16),
                 jax.ShapeDtypeStruct((b, DN_HEADS, DN_DK, DN_DV), F32)]
    scratch = [pltpu.VMEM((HALO + rows, DN_CONV_CH), BF16),
               pltpu.VMEM((rows, DN_V), F32),
               pltpu.VMEM((rows, DN_QK), BF16),
               pltpu.VMEM((rows, DN_QK), BF16),
               pltpu.VMEM((rows, DN_QK), BF16),
               pltpu.VMEM((rows, DN_HEADS * CHUNK), BF16),
               pltpu.VMEM((cpb * DN_HEADS, DN_DV), F32),
               pltpu.VMEM((DN_HEADS, DN_DK, DN_DV), F32)]
    return pl.pallas_call(
        functools.partial(_deltanet_kernel, cpb=cpb),
        grid=(b, steps + 1), in_specs=in_specs, out_specs=out_specs, out_shape=out_shape,
        scratch_shapes=scratch,
        compiler_params=_cparams(("parallel", "arbitrary")), name="deltanet")(
            proj, proj, halo0, small, valid_tab, conv_w, pad_heads(a_log), pad_heads(dt_bias),
            proj, norm_w.reshape(1, DN_DV), s0)


def _layer_retention(h, pos, valid, chunk, s0, p):
    b, s, d = h.shape
    h2 = h.reshape(b * s, d)
    proj = rms_inproj(h2, p["mix_norm_w"], p["w_in"]).reshape(b, s, -1)
    o, s_fin = retention(proj, pos, valid, p["gn_w"], s0, chunk)
    h2 = out_ffn(h2, o.reshape(b * s, RET_V), p["w_out"], p["ffn_norm_w"], p["w_gate"], p["w_up"],
                 p["w_down"])
    return h2.reshape(b, s, d), s_fin


def _layer_deltanet(h, valid, s0, halo0, p, final_w=None, run_tail=True):
    b, s, d = h.shape
    h2 = h.reshape(b * s, d)
    proj, small = rms_inproj(h2, p["mix_norm_w"], p["w_in"], p["w_in_small"])
    proj = proj.reshape(b, s, -1)
    small = small.reshape(b, s, -1)
    o, s_fin = deltanet_mixer(proj, small, valid, halo0, p["conv_w"], p["a_log"], p["dt_bias"],
                              p["norm_w"], s0)
    out = None
    if run_tail:
        out = out_ffn(h2, o.reshape(b * s, DN_V), p["w_out"], p["ffn_norm_w"], p["w_gate"], p["w_up"],
                      p["w_down"], final_w).reshape(b, s, d)
    return out, s_fin, proj


def kernel(x, meta_tokens, mix_norm_w, ffn_norm_w, ret_w_in, ret_gn_w, ret_w_out, dn_w_in, dn_conv_w,
           dn_a_log, dn_dt_bias, dn_norm_w, dn_w_out, ffn_w_gate, ffn_w_up, ffn_w_down, final_norm_w):
    b, s, d = x.shape
    bf = lambda t: t.astype(BF16)
    dn_main = DN_CONV_CH + DN_V
    small_w = jnp.zeros((d, LANES), F32).at[:, :2 * DN_HEADS].set(dn_w_in[0][:, dn_main:])
    p_ret = dict(mix_norm_w=mix_norm_w[0], w_in=bf(ret_w_in[0]), gn_w=ret_gn_w[0], w_out=bf(ret_w_out[0]),
                 ffn_norm_w=ffn_norm_w[0], w_gate=bf(ffn_w_gate[0]), w_up=bf(ffn_w_up[0]),
                 w_down=bf(ffn_w_down[0]))
    p_dn = dict(mix_norm_w=mix_norm_w[1], w_in=bf(dn_w_in[0][:, :dn_main]), w_in_small=bf(small_w),
                conv_w=dn_conv_w[0], a_log=dn_a_log[0], dt_bias=dn_dt_bias[0], norm_w=dn_norm_w[0],
                w_out=bf(dn_w_out[0]), ffn_norm_w=ffn_norm_w[1], w_gate=bf(ffn_w_gate[1]),
                w_up=bf(ffn_w_up[1]), w_down=bf(ffn_w_down[1]))

    h_meta = jnp.concatenate([jnp.zeros((PAD, d), x.dtype), meta_tokens.astype(x.dtype)], axis=0)[None]
    pos_meta = (jnp.arange(CHUNK) - PAD).astype(F32)
    valid_meta = (pos_meta >= 0).astype(F32)
    ret_s0 = jnp.zeros((RET_HEADS, RET_DK, RET_DV), F32)
    dn_s0 = jnp.zeros((DN_HEADS, DN_DK, DN_DV), F32)
    halo_zero = jnp.zeros((HALO, DN_CONV_CH), BF16)
    h_meta, ret_state = _layer_retention(h_meta, pos_meta, valid_meta, CHUNK, ret_s0, p_ret)
    _, dn_state, proj_meta = _layer_deltanet(h_meta, valid_meta, dn_s0, halo_zero, p_dn, run_tail=False)
    halo_meta = proj_meta[0, CHUNK - HALO:, :DN_CONV_CH]

    pos = (jnp.arange(s) + N_META).astype(F32)
    valid = jnp.ones((s,), F32)
    h, _ = _layer_retention(x, pos, valid, RET_CHUNK, ret_state[0], p_ret)
    out, _, _ = _layer_deltanet(h, valid, dn_state[0], halo_meta, p_dn, final_w=final_norm_w)
    return out
```

```python
import functools

import jax
import jax.numpy as jnp
from jax import lax
from jax.experimental import pallas as pl
from jax.experimental.pallas import tpu as pltpu

F32 = jnp.float32
BF16 = jnp.bfloat16

N_META = 16
CHUNK = 64
PAD = CHUNK - N_META
RMS_EPS = 1e-6
RET_HEADS = 4
RET_DK = 256
RET_DV = 512
RET_QK = RET_HEADS * RET_DK
RET_V = RET_HEADS * RET_DV
ROPE_BASE = 10000.0
DN_HEADS = 8
DN_DK = 128
DN_DV = 256
DN_QK = DN_HEADS * DN_DK
DN_V = DN_HEADS * DN_DV
DN_CONV_CH = 2 * DN_QK + DN_V
CONV_K = 4

LANES = 128
SUBLANES = 8
HALO = 16
VMEM_LIMIT_BYTES = 56 * 1024 * 1024
TOKEN_BLOCK = 512
PROJ_N_CHUNK = 1024
FFN_ROW_SPLITS = 2
RET_CHUNK = 256
DN_CHUNKS_PER_STEP = 4


def _cparams(sem):
    return pltpu.CompilerParams(dimension_semantics=sem, vmem_limit_bytes=VMEM_LIMIT_BYTES)


def _resident(shape):
    nd = len(shape)
    return pl.BlockSpec(shape, lambda *_: (0,) * nd, pipeline_mode=pl.Buffered(1))


def _rms(x, w):
    ms = jnp.mean(x * x, axis=-1, keepdims=True)
    return x * lax.rsqrt(ms + RMS_EPS) * w


def _silu(x):
    return x * jax.nn.sigmoid(x)


def _dot(a, b):
    return jnp.dot(a, b, preferred_element_type=F32)


def _dot_nt(a, b):
    return lax.dot_general(a, b, (((1,), (1,)), ((), ())), preferred_element_type=F32)


def _dot_tn(a, b):
    return lax.dot_general(a, b, (((0,), (0,)), ((), ())), preferred_element_type=F32)


def _rms_inproj_kernel(*refs, has_small, has_rope):
    refs = list(refs)
    h_ref, nw_ref, w_ref = refs[:3]
    del refs[:3]
    if has_small:
        ws_ref = refs.pop(0)
    if has_rope:
        cos_ref, sin_ref, valid_ref = refs[:3]
        del refs[:3]
    o_ref = refs.pop(0)
    xn = _rms(h_ref[...], nw_ref[...]).astype(BF16)
    n_total = o_ref.shape[-1]
    for n0 in range(0, n_total, PROJ_N_CHUNK):
        n1 = min(n0 + PROJ_N_CHUNK, n_total)
        res = _dot(xn, w_ref[:, n0:n1])
        if has_rope and n0 < 2 * RET_QK:
            cos, sin = cos_ref[...], sin_ref[...]
            half = RET_DK // 2
            parts = []
            for c0 in range(0, n1 - n0, RET_DK):
                t1, t2 = res[:, c0:c0 + half], res[:, c0 + half:c0 + RET_DK]
                r1, r2 = t1 * cos - t2 * sin, t1 * sin + t2 * cos
                if n0 + c0 >= RET_QK:
                    valid = valid_ref[...]
                    r1 = r1 * (RET_DK ** -0.5) * valid
                    r2 = r2 * (RET_DK ** -0.5) * valid
                parts += [r1, r2]
            res = jnp.concatenate(parts, axis=1)
        o_ref[:, n0:n1] = res.astype(o_ref.dtype)
    if has_small:
        refs[0][...] = _dot(xn, ws_ref[...])


def rms_inproj(h2d, norm_w, w, w_small=None, rope=None):
    m, d = h2d.shape
    n = w.shape[1]
    bm = min(TOKEN_BLOCK, m)
    has_small = w_small is not None
    has_rope = rope is not None
    in_specs = [pl.BlockSpec((bm, d), lambda i: (i, 0)), _resident((1, d)), _resident((d, n))]
    args = [h2d, norm_w.reshape(1, d), w]
    out_shape = [jax.ShapeDtypeStruct((m, n), BF16)]
    out_specs = [pl.BlockSpec((bm, n), lambda i: (i, 0))]
    if has_small:
        ns = w_small.shape[1]
        in_specs.append(_resident((d, ns)))
        args.append(w_small)
        out_shape.append(jax.ShapeDtypeStruct((m, ns), F32))
        out_specs.append(pl.BlockSpec((bm, ns), lambda i: (i, 0)))
    if has_rope:
        assert PROJ_N_CHUNK % RET_DK == 0 and RET_QK % PROJ_N_CHUNK == 0
        blocks_per_seq = rope[0].shape[0] // bm
        in_specs += [pl.BlockSpec((bm, LANES), lambda i: (i % blocks_per_seq, 0))] * 3
        args += list(rope)
    outs = pl.pallas_call(
        functools.partial(_rms_inproj_kernel, has_small=has_small, has_rope=has_rope),
        grid=(m // bm,), in_specs=in_specs, out_specs=out_specs, out_shape=out_shape,
        compiler_params=_cparams(("parallel",)), name="rms_inproj")(*args)
    return outs if has_small else outs[0]


def _out_ffn_kernel(*refs, final, head_dv):
    if final:
        h_ref, o_ref, g_ref, gnw_ref, wo_ref, nw_ref, wg_ref, wu_ref, wd_ref, fw_ref, out_ref = refs
    else:
        h_ref, o_ref, g_ref, gnw_ref, wo_ref, nw_ref, wg_ref, wu_ref, wd_ref, out_ref = refs
    gnw = gnw_ref[...]
    bm = h_ref.shape[0]
    n_sub = FFN_ROW_SPLITS if bm % (FFN_ROW_SPLITS * SUBLANES * 2) == 0 else 1
    outs = []
    for r0 in range(0, bm, bm // n_sub):
        rs = slice(r0, r0 + bm // n_sub)
        h1 = h_ref[rs, :]
        for c0 in range(0, o_ref.shape[1], head_dv):
            x = o_ref[rs, c0:c0 + head_dv].astype(F32)
            xn = x * lax.rsqrt(jnp.mean(x * x, axis=-1, keepdims=True) + RMS_EPS) * gnw
            og = (xn * _silu(g_ref[rs, c0:c0 + head_dv].astype(F32))).astype(BF16)
            h1 = h1 + _dot(og, wo_ref[c0:c0 + head_dv, :])
        xn = _rms(h1, nw_ref[...]).astype(BF16)
        act = (_silu(_dot(xn, wg_ref[...])) * _dot(xn, wu_ref[...])).astype(BF16)
        h2 = h1 + _dot(act, wd_ref[...])
        if final:
            h2 = _rms(h2, fw_ref[...])
        outs.append(h2)
    out_ref[...] = jnp.concatenate(outs, axis=0)


def out_ffn(h2d, o2d, proj2d, gate_blk, gn_w, w_out, norm_w, w_gate, w_up, w_down, final_w=None):
    m, d = h2d.shape
    dv = o2d.shape[1]
    head_dv = gn_w.shape[0]
    f = w_gate.shape[1]
    bm = min(TOKEN_BLOCK, m)
    final = final_w is not None
    row = lambda width, blk=0: pl.BlockSpec((bm, width), lambda i: (i, blk))
    in_specs = [row(d), row(dv), row(dv, gate_blk), _resident((1, head_dv)), _resident((dv, d)),
                _resident((1, d)), _resident((d, f)), _resident((d, f)), _resident((f, d))]
    args = [h2d, o2d, proj2d, gn_w.reshape(1, head_dv), w_out, norm_w.reshape(1, d), w_gate, w_up, w_down]
    if final:
        in_specs.append(_resident((1, d)))
        args.append(final_w.reshape(1, d))
    return pl.pallas_call(
        functools.partial(_out_ffn_kernel, final=final, head_dv=head_dv),
        grid=(m // bm,), in_specs=in_specs, out_specs=row(d),
        out_shape=jax.ShapeDtypeStruct((m, d), F32),
        compiler_params=_cparams(("parallel",)), name="out_ffn")(*args)


def _retention_kernel(qk_ref, v_ref, dmask_ref, xi_ref, zeta_ref, gc_ref, s0_ref, o_ref, sfin_ref,
                      state):
    c = pl.program_id(1)

    @pl.when(c == 0)
    def _():
        state[...] = s0_ref[...]

    heads = range(RET_HEADS)
    wide = lambda tab: jnp.concatenate([tab] * (RET_DK // LANES), axis=1)
    qb = [qk_ref[:, h * RET_DK:(h + 1) * RET_DK] for h in heads]
    kb = [qk_ref[:, RET_QK + h * RET_DK:RET_QK + (h + 1) * RET_DK] for h in heads]
    qx = [(qb[h].astype(F32) * wide(xi_ref[h])).astype(BF16) for h in heads]
    kz = [(kb[h].astype(F32) * wide(zeta_ref[h])).astype(BF16) for h in heads]
    vb = [v_ref[:, h * RET_DV:(h + 1) * RET_DV] for h in heads]
    s_prev = [state[h] for h in heads]
    scores = [(_dot_nt(qb[h], kb[h]) * dmask_ref[h]).astype(BF16) for h in heads]
    o = [_dot(scores[h], vb[h]) + _dot(qx[h], s_prev[h].astype(BF16)) for h in heads]
    s_new = [gc_ref[h, 0:1, :] * s_prev[h] + _dot_tn(kz[h], vb[h]) for h in heads]
    state[...] = jnp.stack(s_new, axis=0)
    o_ref[...] = jnp.concatenate(o, axis=1).astype(o_ref.dtype)

    @pl.when(c == pl.num_programs(1) - 1)
    def _():
        sfin_ref[...] = state[...]


def _retention_tables(chunk):
    log_gamma = jnp.log1p(-jnp.exp2(-5.0 - jnp.arange(RET_HEADS, dtype=F32)))
    idx = jnp.arange(chunk, dtype=F32)
    rel = idx[:, None] - idx[None, :]
    dmask = jnp.where((rel >= 0)[None],
                      jnp.exp(log_gamma[:, None, None] * jnp.maximum(rel, 0.0)), 0.0)
    xi = jnp.exp(log_gamma[:, None] * (idx[None, :] + 1.0))
    zeta = jnp.exp(log_gamma[:, None] * (chunk - 1.0 - idx[None, :]))
    gamma_c = jnp.exp(log_gamma * chunk)
    xi = jnp.broadcast_to(xi[:, :, None], (RET_HEADS, chunk, LANES))
    zeta = jnp.broadcast_to(zeta[:, :, None], (RET_HEADS, chunk, LANES))
    gc = jnp.broadcast_to(gamma_c[:, None, None], (RET_HEADS, SUBLANES, RET_DV))
    return dmask, xi, zeta, gc


def _rope_tables(pos):
    half = RET_DK // 2
    inv_freq = ROPE_BASE ** (-jnp.arange(half, dtype=F32) / half)
    ang = pos[:, None] * inv_freq[None, :]
    return jnp.cos(ang), jnp.sin(ang)


def retention(proj, s0, chunk):
    b, s, _ = proj.shape
    nc = s // chunk
    dmask, xi, zeta, gc = _retention_tables(chunk)
    assert 2 * RET_QK == RET_V
    tok = lambda blk: pl.BlockSpec((None, chunk, RET_V), lambda bi, c: (bi, c, blk))
    whole = lambda *shape: pl.BlockSpec(shape, lambda bi, c: (0,) * len(shape))
    in_specs = [tok(0), tok(1),
                whole(RET_HEADS, chunk, chunk), whole(RET_HEADS, chunk, LANES),
                whole(RET_HEADS, chunk, LANES), whole(RET_HEADS, SUBLANES, RET_DV),
                whole(RET_HEADS, RET_DK, RET_DV)]
    out_specs = [tok(0),
                 pl.BlockSpec((None, RET_HEADS, RET_DK, RET_DV), lambda bi, c: (bi, 0, 0, 0))]
    out_shape = [jax.ShapeDtypeStruct((b, s, RET_V), BF16),
                 jax.ShapeDtypeStruct((b, RET_HEADS, RET_DK, RET_DV), F32)]
    return pl.pallas_call(
        _retention_kernel, grid=(b, nc), in_specs=in_specs, out_specs=out_specs,
        out_shape=out_shape, scratch_shapes=[pltpu.VMEM((RET_HEADS, RET_DK, RET_DV), F32)],
        compiler_params=_cparams(("parallel", "arbitrary")), name="retention")(
            proj, proj, dmask, xi, zeta, gc, s0)


def _split(x):
    hi = x.astype(BF16)
    lo = (x - hi.astype(F32)).astype(BF16)
    return hi, lo


def _pair_block_diag(y, low):
    zero = jnp.zeros_like(y)
    return jnp.concatenate([jnp.where(low, y, zero), jnp.where(low, zero, y)], axis=0)


def _mm_pair(xs, ys, low):
    (xh, xl), (yh, yl) = xs, ys
    lhs = jnp.concatenate([xh, xh, xl], axis=1)
    rhs = jnp.concatenate([_pair_block_diag(yh, low), _pair_block_diag(yl, low),
                           _pair_block_diag(yh, low)], axis=0)
    return _dot(lhs, rhs)


def _unit_lower_inverses(a_list, row, col, low):
    eye = (row == col).astype(F32)
    blk = lambda n: (row // n) == (col // n)
    mm = functools.partial(_mm_pair, low=low)
    sq = lambda xs: [mm(x, x) for x in xs]
    n1 = [jnp.where(blk(16), -a, 0.0) for a in a_list]
    n2 = sq([_split(x) for x in n1])
    n4 = sq([_split(x) for x in n2])
    n8 = sq([_split(x) for x in n4])
    t = [mm(_split(eye + a), _split(eye + b)) for a, b in zip(n1, n2)]
    t = [mm(_split(x), _split(eye + y)) for x, y in zip(t, n4)]
    t = [mm(_split(x), _split(eye + y)) for x, y in zip(t, n8)]
    for n in (32, 64):
        off_mask = blk(n) & ~blk(n // 2)
        ts = [_split(x) for x in t]
        to = [mm(x, _split(jnp.where(off_mask, a, 0.0))) for x, a in zip(ts, a_list)]
        t = [x - mm(_split(y), xs) for x, y, xs in zip(t, to, ts)]
    return [_split(x) for x in t]


def _dn_recurrence(u_s, w_s, qg_s, kd_s, attn_s, eg_s, state, cpb):
    heads = range(DN_HEADS)
    dk = [slice(h * DN_DK, (h + 1) * DN_DK) for h in heads]
    dv = [slice(h * DN_DV, (h + 1) * DN_DV) for h in heads]
    s_cur = [state[h] for h in heads]
    out_rows = []
    for j in range(cpb):
        rs = slice(j * CHUNK, (j + 1) * CHUNK)
        wq = [jnp.concatenate([w_s[rs, dk[h]], qg_s[rs, dk[h]]], axis=0) for h in heads]
        ws = [_dot(wq[h], s_cur[h].astype(BF16)) for h in heads]
        vb = [(u_s[rs, dv[h]] - ws[h][:CHUNK]).astype(BF16) for h in heads]
        o = [ws[h][CHUNK:] + _dot(attn_s[rs, h * CHUNK:(h + 1) * CHUNK], vb[h]) for h in heads]
        s_cur = [s_cur[h] * eg_s[j * DN_HEADS + h:j * DN_HEADS + h + 1, :] + _dot_tn(kd_s[rs, dk[h]], vb[h])
                 for h in heads]
        out_rows.append(jnp.concatenate(o, axis=1))
    state[...] = jnp.stack(s_cur, axis=0)
    return jnp.concatenate(out_rows, axis=0)


def _deltanet_kernel(qkv_ref, halo_ref, halo0_ref, small_ref, valid_ref, convw_ref, alog_ref, dtb_ref,
                     s0_ref, o_ref, sfin_ref,
                     xs, u_ref, w_ref, qg_ref, kd_ref, attn_ref, eg_ref, state, *, cpb):
    c = pl.program_id(1)
    last_work = pl.num_programs(1) - 2

    @pl.when(c == 0)
    def _():
        for ref in (u_ref, w_ref, qg_ref, kd_ref, attn_ref, eg_ref):
            ref[...] = jnp.zeros(ref.shape, ref.dtype)

    @pl.when(c <= 1)
    def _():
        state[...] = s0_ref[...]

    @pl.when(jnp.minimum(c, last_work) == 0)
    def _():
        xs[0:HALO, :] = halo0_ref[...]

    @pl.when(jnp.minimum(c, last_work) > 0)
    def _():
        xs[0:HALO, :] = halo_ref[...]

    o_ref[...] = _dn_recurrence(u_ref, w_ref, qg_ref, kd_ref, attn_ref, eg_ref, state,
                                cpb).astype(o_ref.dtype)

    xs[HALO:, :] = qkv_ref[...] * jnp.concatenate([valid_ref[...].astype(BF16)] * (DN_CONV_CH // LANES),
                                                  axis=1)

    sel_r = lax.broadcasted_iota(jnp.int32, ((CONV_K - 1) * CHUNK, HALO + CHUNK), 0)
    sel_c = lax.broadcasted_iota(jnp.int32, ((CONV_K - 1) * CHUNK, HALO + CHUNK), 1)
    shift_sel = (sel_c == (sel_r & (CHUNK - 1)) + (sel_r // CHUNK) + (HALO - (CONV_K - 1))).astype(BF16)

    def conv_silu(j, lo, width):
        r0 = j * CHUNK
        delayed = _dot(shift_sel, xs[r0:r0 + HALO + CHUNK, lo:lo + width])
        acc = (xs[r0 + HALO:r0 + HALO + CHUNK, lo:lo + width].astype(F32)
               * convw_ref[CONV_K - 1:CONV_K, lo:lo + width])
        for i in range(CONV_K - 1):
            acc = acc + delayed[i * CHUNK:(i + 1) * CHUNK] * convw_ref[i:i + 1, lo:lo + width]
        return _silu(acc)

    row = lax.broadcasted_iota(jnp.int32, (CHUNK, CHUNK), 0)
    col = lax.broadcasted_iota(jnp.int32, (CHUNK, CHUNK), 1)
    tri_l = (row >= col).astype(BF16)
    tri_u = (row <= col).astype(BF16)

    def split3(x):
        hi, lo = _split(x)
        lo2 = (x - hi.astype(F32) - lo.astype(F32)).astype(BF16)
        return hi, lo, lo2

    beta_all, gam_all, gam_rows, egam_all, ekd_all = [], [], [], [], []
    for j in range(cpb):
        rs = slice(j * CHUNK, (j + 1) * CHUNK)
        valid = valid_ref[rs, :]
        small = small_ref[rs, :]
        beta_all.append(jax.nn.sigmoid(small) * valid)
        g_all = -jnp.exp(alog_ref[...]) * jax.nn.softplus(small + dtb_ref[...]) * valid
        gam = _dot(jnp.concatenate([tri_l] * 3, axis=1), jnp.concatenate(split3(g_all), axis=0))
        g_rows = g_all.T[DN_HEADS:2 * DN_HEADS, :]
        gam_rows.append(_dot(jnp.concatenate(split3(g_rows), axis=1), jnp.concatenate([tri_u] * 3, axis=0)))
        gam_all.append(gam)
        egam_all.append(jnp.exp(gam))
        ekd_all.append(jnp.exp(gam[CHUNK - 1:CHUNK, :] - gam))

    pairs = range(DN_HEADS // 2)
    units = [(j, p) for j in range(cpb) for p in pairs]
    lane_c = lax.broadcasted_iota(jnp.int32, (CHUNK, 2 * CHUNK), 1)
    prow = lax.broadcasted_iota(jnp.int32, (CHUNK, 2 * CHUNK), 0)
    pcol = lane_c & (CHUNK - 1)
    low_c = lane_c < CHUNK
    low_k = lax.broadcasted_iota(jnp.int32, (CHUNK, 2 * DN_DK), 1) < DN_DK
    p_incl = prow >= pcol
    p_strict = prow > pcol

    def head_cols(x_all, off, p, low):
        a = x_all[:, off + 2 * p:off + 2 * p + 1]
        b = x_all[:, off + 2 * p + 1:off + 2 * p + 2]
        return jnp.where(low, a, b)

    def l2n(x):
        halves = [x[:, :DN_DK], x[:, DN_DK:]]
        return jnp.concatenate(
            [y * lax.rsqrt(jnp.sum(y * y, axis=-1, keepdims=True) + RMS_EPS) for y in halves], axis=1)

    q = [l2n(conv_silu(j, p * 2 * DN_DK, 2 * DN_DK)) * (DN_DK ** -0.5) for j, p in units]
    k = [l2n(conv_silu(j, DN_QK + p * 2 * DN_DK, 2 * DN_DK)) for j, p in units]
    v = [conv_silu(j, 2 * DN_QK + p * 2 * DN_DV, 2 * DN_DV) for j, p in units]
    beta_c = [head_cols(beta_all[j], 0, p, low_c) for j, p in units]
    gcol = [head_cols(gam_all[j], DN_HEADS, p, low_c) for j, p in units]
    grow = [jnp.concatenate([gam_rows[j][2 * p:2 * p + 1, :], gam_rows[j][2 * p + 1:2 * p + 2, :]], axis=1)
            for j, p in units]
    decay = [jnp.exp(jnp.where(p_incl, gc - gr, -jnp.inf)) for gc, gr in zip(gcol, grow)]
    kb = [x.astype(BF16) for x in k]
    zk = jnp.zeros((CHUNK, 2 * DN_DK), BF16)
    kq = [_dot_nt(jnp.concatenate([kbp, qp.astype(BF16)], axis=0),
                  jnp.concatenate([jnp.where(low_k, kbp, zk), jnp.where(low_k, zk, kbp)], axis=0))
          for kbp, qp in zip(kb, q)]
    a_mat = [jnp.where(p_strict, b * x[:CHUNK] * d, 0.0) for b, x, d in zip(beta_c, kq, decay)]
    t = _unit_lower_inverses(a_mat, prow, pcol, low_c)

    zr = jnp.zeros((CHUNK, DN_DV + DN_DK), F32)
    uw = []
    for n, (j, p) in enumerate(units):
        r = []
        for i in range(2):
            h = 2 * p + i
            b = beta_all[j][:, h:h + 1]
            e = egam_all[j][:, DN_HEADS + h:DN_HEADS + h + 1]
            r.append(jnp.concatenate([v[n][:, i * DN_DV:(i + 1) * DN_DV] * b,
                                      k[n][:, i * DN_DK:(i + 1) * DN_DK] * (b * e)], axis=1))
        rhs = jnp.concatenate([jnp.concatenate([r[0], zr], axis=1),
                               jnp.concatenate([zr, r[1]], axis=1)], axis=0).astype(BF16)
        th, tl = t[n]
        uw.append(_dot(jnp.concatenate([th, tl], axis=1), jnp.concatenate([rhs, rhs], axis=0)))

    dvk = DN_DV + DN_DK
    n_pairs = len(pairs)
    rows = lambda per_chunk: jnp.concatenate([per_chunk(j) for j in range(cpb)], axis=0)
    lanes = lambda parts: jnp.concatenate(parts, axis=1)
    u_ref[...] = rows(lambda j: lanes([uw[j * n_pairs + p][:, i * dvk:i * dvk + DN_DV]
                                       for p in pairs for i in range(2)]))
    w_ref[...] = rows(lambda j: lanes([uw[j * n_pairs + p][:, i * dvk + DN_DV:(i + 1) * dvk]
                                       for p in pairs for i in range(2)])).astype(BF16)
    qg_ref[...] = rows(lambda j: lanes([q[j * n_pairs + p] * head_cols(egam_all[j], DN_HEADS, p, low_k)
                                        for p in pairs])).astype(BF16)
    kd_ref[...] = rows(lambda j: lanes([k[j * n_pairs + p] * head_cols(ekd_all[j], DN_HEADS, p, low_k)
                                        for p in pairs])).astype(BF16)
    attn_ref[...] = rows(lambda j: lanes([kq[j * n_pairs + p][CHUNK:] * decay[j * n_pairs + p]
                                          for p in pairs])).astype(BF16)
    eg_ref[...] = rows(lambda j: jnp.broadcast_to(jnp.exp(gam_rows[j][:, CHUNK - 1:CHUNK]),
                                                  (DN_HEADS, DN_DV)))

    @pl.when(c == pl.num_programs(1) - 1)
    def _():
        sfin_ref[...] = state[...]


def deltanet_mixer(proj, small, valid, halo0, conv_w, a_log, dt_bias, s0):
    b, s, _ = proj.shape
    nc = s // CHUNK
    cpb = DN_CHUNKS_PER_STEP if nc % DN_CHUNKS_PER_STEP == 0 else 1
    rows = cpb * CHUNK
    steps = nc // cpb
    hb = rows // HALO
    valid_tab = jnp.broadcast_to(valid[:, None], (s, LANES))
    pad_heads = lambda v: jnp.zeros((1, LANES), F32).at[0, DN_HEADS:2 * DN_HEADS].set(v)
    cur = lambda c: jnp.minimum(c, steps - 1)
    prev = lambda c: jnp.maximum(c - 1, 0)
    const = lambda *shape: pl.BlockSpec(shape, lambda bi, c: (0,) * len(shape))
    in_specs = [pl.BlockSpec((None, rows, DN_CONV_CH), lambda bi, c: (bi, cur(c), 0)),
                pl.BlockSpec((None, HALO, DN_CONV_CH), lambda bi, c: (bi, jnp.maximum(cur(c) * hb - 1, 0), 0)),
                const(HALO, DN_CONV_CH),
                pl.BlockSpec((None, rows, LANES), lambda bi, c: (bi, cur(c), 0)),
                pl.BlockSpec((rows, LANES), lambda bi, c: (cur(c), 0)),
                const(CONV_K, DN_CONV_CH), const(1, LANES), const(1, LANES),
                const(DN_HEADS, DN_DK, DN_DV)]
    out_specs = [pl.BlockSpec((None, rows, DN_V), lambda bi, c: (bi, prev(c), 0)),
                 pl.BlockSpec((None, DN_HEADS, DN_DK, DN_DV), lambda bi, c: (bi, 0, 0, 0))]
    out_shape = [jax.ShapeDtypeStruct((b, s, DN_V), BF16),
                 jax.ShapeDtypeStruct((b, DN_HEADS, DN_DK, DN_DV), F32)]
    scratch = [pltpu.VMEM((HALO + rows, DN_CONV_CH), BF16),
               pltpu.VMEM((rows, DN_V), F32),
               pltpu.VMEM((rows, DN_QK), BF16),
               pltpu.VMEM((rows, DN_QK), BF16),
               pltpu.VMEM((rows, DN_QK), BF16),
               pltpu.VMEM((rows, DN_HEADS * CHUNK), BF16),
               pltpu.VMEM((cpb * DN_HEADS, DN_DV), F32),
               pltpu.VMEM((DN_HEADS, DN_DK, DN_DV), F32)]
    return pl.pallas_call(
        functools.partial(_deltanet_kernel, cpb=cpb),
        grid=(b, steps + 1), in_specs=in_specs, out_specs=out_specs, out_shape=out_shape,
        scratch_shapes=scratch,
        compiler_params=_cparams(("parallel", "arbitrary")), name="deltanet")(
            proj, proj, halo0, small, valid_tab, conv_w, pad_heads(a_log), pad_heads(dt_bias), s0)


def _layer_retention(h, pos, valid, chunk, s0, p):
    b, s, d = h.shape
    h2 = h.reshape(b * s, d)
    cos, sin = _rope_tables(pos)
    valid_tab = jnp.broadcast_to(valid[:, None], (s, LANES))
    proj2 = rms_inproj(h2, p["mix_norm_w"], p["w_in"], rope=(cos, sin, valid_tab))
    o, s_fin = retention(proj2.reshape(b, s, -1), s0, chunk)
    gate_blk = (2 * RET_QK + RET_V) // RET_V
    h2 = out_ffn(h2, o.reshape(b * s, RET_V), proj2, gate_blk, p["gn_w"], p["w_out"], p["ffn_norm_w"],
                 p["w_gate"], p["w_up"], p["w_down"])
    return h2.reshape(b, s, d), s_fin


def _layer_deltanet(h, valid, s0, halo0, p, final_w=None, run_tail=True):
    b, s, d = h.shape
    h2 = h.reshape(b * s, d)
    proj2, small = rms_inproj(h2, p["mix_norm_w"], p["w_in"], p["w_in_small"])
    proj = proj2.reshape(b, s, -1)
    small = small.reshape(b, s, -1)
    o, s_fin = deltanet_mixer(proj, small, valid, halo0, p["conv_w"], p["a_log"], p["dt_bias"], s0)
    out = None
    if run_tail:
        gate_blk = DN_CONV_CH // DN_V
        out = out_ffn(h2, o.reshape(b * s, DN_V), proj2, gate_blk, p["norm_w"], p["w_out"],
                      p["ffn_norm_w"], p["w_gate"], p["w_up"], p["w_down"], final_w).reshape(b, s, d)
    return out, s_fin, proj


def kernel(x, meta_tokens, mix_norm_w, ffn_norm_w, ret_w_in, ret_gn_w, ret_w_out, dn_w_in, dn_conv_w,
           dn_a_log, dn_dt_bias, dn_norm_w, dn_w_out, ffn_w_gate, ffn_w_up, ffn_w_down, final_norm_w):
    b, s, d = x.shape
    bf = lambda t: t.astype(BF16)
    dn_main = DN_CONV_CH + DN_V
    small_w = jnp.zeros((d, LANES), F32).at[:, :2 * DN_HEADS].set(dn_w_in[0][:, dn_main:])
    p_ret = dict(mix_norm_w=mix_norm_w[0], w_in=bf(ret_w_in[0]), gn_w=ret_gn_w[0], w_out=bf(ret_w_out[0]),
                 ffn_norm_w=ffn_norm_w[0], w_gate=bf(ffn_w_gate[0]), w_up=bf(ffn_w_up[0]),
                 w_down=bf(ffn_w_down[0]))
    p_dn = dict(mix_norm_w=mix_norm_w[1], w_in=bf(dn_w_in[0][:, :dn_main]), w_in_small=bf(small_w),
                conv_w=dn_conv_w[0], a_log=dn_a_log[0], dt_bias=dn_dt_bias[0], norm_w=dn_norm_w[0],
                w_out=bf(dn_w_out[0]), ffn_norm_w=ffn_norm_w[1], w_gate=bf(ffn_w_gate[1]),
                w_up=bf(ffn_w_up[1]), w_down=bf(ffn_w_down[1]))

    h_meta = jnp.concatenate([jnp.zeros((PAD, d), x.dtype), meta_tokens.astype(x.dtype)], axis=0)[None]
    pos_meta = (jnp.arange(CHUNK) - PAD).astype(F32)
    valid_meta = (pos_meta >= 0).astype(F32)
    ret_s0 = jnp.zeros((RET_HEADS, RET_DK, RET_DV), F32)
    dn_s0 = jnp.zeros((DN_HEADS, DN_DK, DN_DV), F32)
    halo_zero = jnp.zeros((HALO, DN_CONV_CH), BF16)
    h_meta, ret_state = _layer_retention(h_meta, pos_meta, valid_meta, CHUNK, ret_s0, p_ret)
    _, dn_state, proj_meta = _layer_deltanet(h_meta, valid_meta, dn_s0, halo_zero, p_dn, run_tail=False)
    halo_meta = proj_meta[0, CHUNK - HALO:, :DN_CONV_CH]

    pos = (jnp.arange(s) + N_META).astype(F32)
    valid = jnp.ones((s,), F32)
    h, _ = _layer_retention(x, pos, valid, RET_CHUNK, ret_state[0], p_ret)
    out, _, _ = _layer_deltanet(h, valid, dn_state[0], halo_meta, p_dn, final_w=final_norm_w)
    return out
```

```python
import functools

import jax
import jax.numpy as jnp
from jax import lax
from jax.experimental import pallas as pl
from jax.experimental.pallas import tpu as pltpu

F32 = jnp.float32
BF16 = jnp.bfloat16

N_META = 16
CHUNK = 64
PAD = CHUNK - N_META
RMS_EPS = 1e-6
RET_HEADS = 4
RET_DK = 256
RET_DV = 512
RET_QK = RET_HEADS * RET_DK
RET_V = RET_HEADS * RET_DV
ROPE_BASE = 10000.0
DN_HEADS = 8
DN_DK = 128
DN_DV = 256
DN_QK = DN_HEADS * DN_DK
DN_V = DN_HEADS * DN_DV
DN_CONV_CH = 2 * DN_QK + DN_V
CONV_K = 4

LANES = 128
SUBLANES = 8
VMEM_LIMIT_BYTES = 56 * 1024 * 1024
TOKEN_BLOCK = 512
PROJ_N_CHUNK = 1024
FFN_ROW_SPLITS = 2
RET_CHUNK = 256
DN_CHUNKS_PER_STEP = 4


def _cparams(sem):
    return pltpu.CompilerParams(dimension_semantics=sem, vmem_limit_bytes=VMEM_LIMIT_BYTES)


def _resident(shape):
    nd = len(shape)
    return pl.BlockSpec(shape, lambda *_: (0,) * nd, pipeline_mode=pl.Buffered(1))


def _rms(x, w):
    ms = jnp.mean(x * x, axis=-1, keepdims=True)
    return x * lax.rsqrt(ms + RMS_EPS) * w


def _silu(x):
    h = 0.5 * x
    return h + h * jnp.tanh(h)


def _dot(a, b):
    return jnp.dot(a, b, preferred_element_type=F32)


def _dot_nt(a, b):
    return lax.dot_general(a, b, (((1,), (1,)), ((), ())), preferred_element_type=F32)


def _dot_tn(a, b):
    return lax.dot_general(a, b, (((0,), (0,)), ((), ())), preferred_element_type=F32)


def _rms_inproj_kernel(*refs, has_rope):
    if has_rope:
        h_ref, nw_ref, w_ref, cos_ref, sin_ref, valid_ref, o_ref = refs
    else:
        h_ref, nw_ref, w_ref, o_ref = refs
    xn = _rms(h_ref[...], nw_ref[...]).astype(BF16)
    n_total = o_ref.shape[-1]
    for n0 in range(0, n_total, PROJ_N_CHUNK):
        n1 = min(n0 + PROJ_N_CHUNK, n_total)
        res = _dot(xn, w_ref[:, n0:n1])
        if has_rope and n0 < 2 * RET_QK:
            cos, sin = cos_ref[...], sin_ref[...]
            half = RET_DK // 2
            parts = []
            for c0 in range(0, n1 - n0, RET_DK):
                t1, t2 = res[:, c0:c0 + half], res[:, c0 + half:c0 + RET_DK]
                r1, r2 = t1 * cos - t2 * sin, t1 * sin + t2 * cos
                if n0 + c0 >= RET_QK:
                    valid = valid_ref[...]
                    r1 = r1 * (RET_DK ** -0.5) * valid
                    r2 = r2 * (RET_DK ** -0.5) * valid
                parts += [r1, r2]
            res = jnp.concatenate(parts, axis=1)
        o_ref[:, n0:n1] = res.astype(o_ref.dtype)


def rms_inproj(h2d, norm_w, w, rope=None):
    m, d = h2d.shape
    n = w.shape[1]
    bm = min(TOKEN_BLOCK, m)
    has_rope = rope is not None
    in_specs = [pl.BlockSpec((bm, d), lambda i: (i, 0)), _resident((1, d)), _resident((d, n))]
    args = [h2d, norm_w.reshape(1, d), w]
    if has_rope:
        assert PROJ_N_CHUNK % RET_DK == 0 and RET_QK % PROJ_N_CHUNK == 0
        blocks_per_seq = rope[0].shape[0] // bm
        in_specs += [pl.BlockSpec((bm, LANES), lambda i: (i % blocks_per_seq, 0))] * 3
        args += list(rope)
    return pl.pallas_call(
        functools.partial(_rms_inproj_kernel, has_rope=has_rope),
        grid=(m // bm,), in_specs=in_specs, out_specs=pl.BlockSpec((bm, n), lambda i: (i, 0)),
        out_shape=jax.ShapeDtypeStruct((m, n), BF16),
        compiler_params=_cparams(("parallel",)), name="rms_inproj")(*args)


def _dn_inproj_kernel(h_ref, nw_ref, w_ref, ws_ref, convw_ref, halo0_ref, valid_ref,
                      o_ref, os_ref, tail_ref, carry, *, blocks_per_seq):
    i = pl.program_id(0)

    @pl.when(i % blocks_per_seq == 0)
    def _():
        carry[...] = halo0_ref[...]

    xn = _rms(h_ref[...], nw_ref[...]).astype(BF16)
    bm = h_ref.shape[0]
    valid = valid_ref[...]
    n_total = o_ref.shape[-1]
    chunks = [(n0, min(n0 + PROJ_N_CHUNK, n_total)) for n0 in range(0, n_total, PROJ_N_CHUNK)]
    res_next = _dot(xn, w_ref[:, chunks[0][0]:chunks[0][1]])
    outs = []
    for ci, (n0, n1) in enumerate(chunks):
        res = res_next
        if ci + 1 < len(chunks):
            res_next = _dot(xn, w_ref[:, chunks[ci + 1][0]:chunks[ci + 1][1]])
        if n0 < DN_CONV_CH:
            x = res * jnp.concatenate([valid] * ((n1 - n0) // LANES), axis=1)
            ext = jnp.concatenate([carry[:, n0:n1], x], axis=0)
            acc = x * convw_ref[CONV_K - 1:CONV_K, n0:n1]
            for j in range(CONV_K - 1):
                delayed = pltpu.roll(ext, shift=CONV_K - 1 - j, axis=0)[SUBLANES:]
                acc = acc + delayed * convw_ref[j:j + 1, n0:n1]
            carry[:, n0:n1] = x[bm - SUBLANES:, :]
            res = _silu(acc)
            if n0 < 2 * DN_QK:
                parts = []
                for c0 in range(0, n1 - n0, DN_DK):
                    y = res[:, c0:c0 + DN_DK]
                    y = y * lax.rsqrt(jnp.sum(y * y, axis=-1, keepdims=True) + RMS_EPS)
                    parts.append(y * (DN_DK ** -0.5) if n0 < DN_QK else y)
                res = jnp.concatenate(parts, axis=1)
        outs.append(res.astype(o_ref.dtype))
    o_ref[...] = jnp.concatenate(outs, axis=1)
    os_ref[...] = _dot(xn, ws_ref[...])

    @pl.when(i == pl.num_programs(0) - 1)
    def _():
        tail_ref[...] = carry[...]


def dn_inproj(h2d, norm_w, w, w_small, conv_w, halo0, valid_tab):
    m, d = h2d.shape
    n = w.shape[1]
    ns = w_small.shape[1]
    bm = min(TOKEN_BLOCK, m)
    assert DN_QK % PROJ_N_CHUNK == 0 and DN_CONV_CH % PROJ_N_CHUNK == 0 and PROJ_N_CHUNK % DN_DK == 0
    blocks_per_seq = valid_tab.shape[0] // bm
    row = lambda width: pl.BlockSpec((bm, width), lambda i: (i, 0))
    in_specs = [row(d), _resident((1, d)), _resident((d, n)), _resident((d, ns)),
                _resident((CONV_K, DN_CONV_CH)), _resident((SUBLANES, DN_CONV_CH)),
                pl.BlockSpec((bm, LANES), lambda i: (i % blocks_per_seq, 0))]
    out_specs = [row(n), row(ns), pl.BlockSpec((SUBLANES, DN_CONV_CH), lambda i: (0, 0))]
    out_shape = [jax.ShapeDtypeStruct((m, n), BF16), jax.ShapeDtypeStruct((m, ns), F32),
                 jax.ShapeDtypeStruct((SUBLANES, DN_CONV_CH), F32)]
    return pl.pallas_call(
        functools.partial(_dn_inproj_kernel, blocks_per_seq=blocks_per_seq),
        grid=(m // bm,), in_specs=in_specs, out_specs=out_specs, out_shape=out_shape,
        scratch_shapes=[pltpu.VMEM((SUBLANES, DN_CONV_CH), F32)],
        compiler_params=_cparams(("arbitrary",)), name="dn_inproj")(
            h2d, norm_w.reshape(1, d), w, w_small, conv_w, halo0, valid_tab)


def _out_ffn_kernel(*refs, final, head_dv):
    if final:
        h_ref, o_ref, g_ref, gnw_ref, wo_ref, nw_ref, wg_ref, wu_ref, wd_ref, fw_ref, out_ref = refs
    else:
        h_ref, o_ref, g_ref, gnw_ref, wo_ref, nw_ref, wg_ref, wu_ref, wd_ref, out_ref = refs
    gnw = gnw_ref[...]
    bm = h_ref.shape[0]
    n_sub = FFN_ROW_SPLITS if bm % (FFN_ROW_SPLITS * SUBLANES * 2) == 0 else 1
    outs = []
    for r0 in range(0, bm, bm // n_sub):
        rs = slice(r0, r0 + bm // n_sub)
        h1 = h_ref[rs, :]
        for c0 in range(0, o_ref.shape[1], head_dv):
            x = o_ref[rs, c0:c0 + head_dv].astype(F32)
            xn = x * lax.rsqrt(jnp.mean(x * x, axis=-1, keepdims=True) + RMS_EPS) * gnw
            og = (xn * _silu(g_ref[rs, c0:c0 + head_dv].astype(F32))).astype(BF16)
            h1 = h1 + _dot(og, wo_ref[c0:c0 + head_dv, :])
        xn = _rms(h1, nw_ref[...]).astype(BF16)
        act = (_silu(_dot(xn, wg_ref[...])) * _dot(xn, wu_ref[...])).astype(BF16)
        h2 = h1 + _dot(act, wd_ref[...])
        if final:
            h2 = _rms(h2, fw_ref[...])
        outs.append(h2)
    out_ref[...] = jnp.concatenate(outs, axis=0)


def out_ffn(h2d, o2d, proj2d, gate_blk, gn_w, w_out, norm_w, w_gate, w_up, w_down, final_w=None):
    m, d = h2d.shape
    dv = o2d.shape[1]
    head_dv = gn_w.shape[0]
    f = w_gate.shape[1]
    bm = min(TOKEN_BLOCK, m)
    final = final_w is not None
    row = lambda width, blk=0: pl.BlockSpec((bm, width), lambda i: (i, blk))
    in_specs = [row(d), row(dv), row(dv, gate_blk), _resident((1, head_dv)), _resident((dv, d)),
                _resident((1, d)), _resident((d, f)), _resident((d, f)), _resident((f, d))]
    args = [h2d, o2d, proj2d, gn_w.reshape(1, head_dv), w_out, norm_w.reshape(1, d), w_gate, w_up, w_down]
    if final:
        in_specs.append(_resident((1, d)))
        args.append(final_w.reshape(1, d))
    return pl.pallas_call(
        functools.partial(_out_ffn_kernel, final=final, head_dv=head_dv),
        grid=(m // bm,), in_specs=in_specs, out_specs=row(d),
        out_shape=jax.ShapeDtypeStruct((m, d), F32),
        compiler_params=_cparams(("parallel",)), name="out_ffn")(*args)


def _retention_kernel(qk_ref, v_ref, dmask_ref, xi_ref, zeta_ref, gc_ref, s0_ref, o_ref, sfin_ref,
                      state):
    c = pl.program_id(1)

    @pl.when(c == 0)
    def _():
        state[...] = s0_ref[...]

    heads = range(RET_HEADS)
    wide = lambda tab: jnp.concatenate([tab] * (RET_DK // LANES), axis=1)
    qb = [qk_ref[:, h * RET_DK:(h + 1) * RET_DK] for h in heads]
    kb = [qk_ref[:, RET_QK + h * RET_DK:RET_QK + (h + 1) * RET_DK] for h in heads]
    qx = [(qb[h].astype(F32) * wide(xi_ref[h])).astype(BF16) for h in heads]
    kz = [(kb[h].astype(F32) * wide(zeta_ref[h])).astype(BF16) for h in heads]
    vb = [v_ref[:, h * RET_DV:(h + 1) * RET_DV] for h in heads]
    s_prev = [state[h] for h in heads]
    scores = [(_dot_nt(qb[h], kb[h]) * dmask_ref[h]).astype(BF16) for h in heads]
    o = [_dot(scores[h], vb[h]) + _dot(qx[h], s_prev[h].astype(BF16)) for h in heads]
    s_new = [gc_ref[h, 0:1, :] * s_prev[h] + _dot_tn(kz[h], vb[h]) for h in heads]
    state[...] = jnp.stack(s_new, axis=0)
    o_ref[...] = jnp.concatenate(o, axis=1).astype(o_ref.dtype)

    @pl.when(c == pl.num_programs(1) - 1)
    def _():
        sfin_ref[...] = state[...]


def _retention_tables(chunk):
    log_gamma = jnp.log1p(-jnp.exp2(-5.0 - jnp.arange(RET_HEADS, dtype=F32)))
    idx = jnp.arange(chunk, dtype=F32)
    rel = idx[:, None] - idx[None, :]
    dmask = jnp.where((rel >= 0)[None],
                      jnp.exp(log_gamma[:, None, None] * jnp.maximum(rel, 0.0)), 0.0)
    xi = jnp.exp(log_gamma[:, None] * (idx[None, :] + 1.0))
    zeta = jnp.exp(log_gamma[:, None] * (chunk - 1.0 - idx[None, :]))
    gamma_c = jnp.exp(log_gamma * chunk)
    xi = jnp.broadcast_to(xi[:, :, None], (RET_HEADS, chunk, LANES))
    zeta = jnp.broadcast_to(zeta[:, :, None], (RET_HEADS, chunk, LANES))
    gc = jnp.broadcast_to(gamma_c[:, None, None], (RET_HEADS, SUBLANES, RET_DV))
    return dmask, xi, zeta, gc


def _rope_tables(pos):
    half = RET_DK // 2
    inv_freq = ROPE_BASE ** (-jnp.arange(half, dtype=F32) / half)
    ang = pos[:, None] * inv_freq[None, :]
    return jnp.cos(ang), jnp.sin(ang)


def retention(proj, s0, chunk):
    b, s, _ = proj.shape
    nc = s // chunk
    dmask, xi, zeta, gc = _retention_tables(chunk)
    assert 2 * RET_QK == RET_V
    tok = lambda blk: pl.BlockSpec((None, chunk, RET_V), lambda bi, c: (bi, c, blk))
    whole = lambda *shape: pl.BlockSpec(shape, lambda bi, c: (0,) * len(shape))
    in_specs = [tok(0), tok(1),
                whole(RET_HEADS, chunk, chunk), whole(RET_HEADS, chunk, LANES),
                whole(RET_HEADS, chunk, LANES), whole(RET_HEADS, SUBLANES, RET_DV),
                whole(RET_HEADS, RET_DK, RET_DV)]
    out_specs = [tok(0),
                 pl.BlockSpec((None, RET_HEADS, RET_DK, RET_DV), lambda bi, c: (bi, 0, 0, 0))]
    out_shape = [jax.ShapeDtypeStruct((b, s, RET_V), BF16),
                 jax.ShapeDtypeStruct((b, RET_HEADS, RET_DK, RET_DV), F32)]
    return pl.pallas_call(
        _retention_kernel, grid=(b, nc), in_specs=in_specs, out_specs=out_specs,
        out_shape=out_shape, scratch_shapes=[pltpu.VMEM((RET_HEADS, RET_DK, RET_DV), F32)],
        compiler_params=_cparams(("parallel", "arbitrary")), name="retention")(
            proj, proj, dmask, xi, zeta, gc, s0)


def _split(x):
    hi = x.astype(BF16)
    lo = (x - hi.astype(F32)).astype(BF16)
    return hi, lo


def _pair_block_diag(y, low):
    zero = jnp.zeros_like(y)
    return jnp.concatenate([jnp.where(low, y, zero), jnp.where(low, zero, y)], axis=0)


def _mm_pair(xs, ys, low):
    (xh, xl), (yh, yl) = xs, ys
    lhs = jnp.concatenate([xh, xh, xl], axis=1)
    rhs = jnp.concatenate([_pair_block_diag(yh, low), _pair_block_diag(yl, low),
                           _pair_block_diag(yh, low)], axis=0)
    return _dot(lhs, rhs)


def _unit_lower_inverses(a_list, row, col, low):
    eye = (row == col).astype(F32)
    blk = lambda n: (row // n) == (col // n)
    mm = functools.partial(_mm_pair, low=low)
    sq = lambda xs: [mm(x, x) for x in xs]
    n1 = [jnp.where(blk(16), -a, 0.0) for a in a_list]
    n2 = sq([_split(x) for x in n1])
    n4 = sq([_split(x) for x in n2])
    n8 = sq([_split(x) for x in n4])
    t = [mm(_split(eye + a), _split(eye + b)) for a, b in zip(n1, n2)]
    t = [mm(_split(x), _split(eye + y)) for x, y in zip(t, n4)]
    t = [mm(_split(x), _split(eye + y)) for x, y in zip(t, n8)]
    for n in (32, 64):
        off_mask = blk(n) & ~blk(n // 2)
        ts = [_split(x) for x in t]
        to = [mm(x, _split(jnp.where(off_mask, a, 0.0))) for x, a in zip(ts, a_list)]
        t = [x - mm(_split(y), xs) for x, y, xs in zip(t, to, ts)]
    return [_split(x) for x in t]


def _dn_recurrence(u_s, w_s, qg_s, kd_s, attn_s, eg_s, state, cpb):
    heads = range(DN_HEADS)
    dk = [slice(h * DN_DK, (h + 1) * DN_DK) for h in heads]
    dv = [slice(h * DN_DV, (h + 1) * DN_DV) for h in heads]
    s_cur = [state[h] for h in heads]
    out_rows = []
    for j in range(cpb):
        rs = slice(j * CHUNK, (j + 1) * CHUNK)
        wq = [jnp.concatenate([w_s[rs, dk[h]], qg_s[rs, dk[h]]], axis=0) for h in heads]
        ws = [_dot(wq[h], s_cur[h].astype(BF16)) for h in heads]
        vb = [(u_s[rs, dv[h]] - ws[h][:CHUNK]).astype(BF16) for h in heads]
        o = [ws[h][CHUNK:] + _dot(attn_s[rs, h * CHUNK:(h + 1) * CHUNK], vb[h]) for h in heads]
        s_cur = [s_cur[h] * eg_s[j * DN_HEADS + h:j * DN_HEADS + h + 1, :] + _dot_tn(kd_s[rs, dk[h]], vb[h])
                 for h in heads]
        out_rows.append(jnp.concatenate(o, axis=1))
    state[...] = jnp.stack(s_cur, axis=0)
    return jnp.concatenate(out_rows, axis=0)


def _deltanet_kernel(qkv_ref, small_ref, valid_ref, alog_ref, dtb_ref, s0_ref, o_ref, sfin_ref,
                     u_ref, w_ref, qg_ref, kd_ref, attn_ref, eg_ref, state, *, cpb):
    c = pl.program_id(1)

    @pl.when(c == 0)
    def _():
        for ref in (u_ref, w_ref, qg_ref, kd_ref, attn_ref, eg_ref):
            ref[...] = jnp.zeros(ref.shape, ref.dtype)

    @pl.when(c <= 1)
    def _():
        state[...] = s0_ref[...]

    o_ref[...] = _dn_recurrence(u_ref, w_ref, qg_ref, kd_ref, attn_ref, eg_ref, state,
                                cpb).astype(o_ref.dtype)

    row = lax.broadcasted_iota(jnp.int32, (CHUNK, CHUNK), 0)
    col = lax.broadcasted_iota(jnp.int32, (CHUNK, CHUNK), 1)
    tri_l = (row >= col).astype(BF16)
    tri_u = (row <= col).astype(BF16)

    def split3(x):
        hi, lo = _split(x)
        lo2 = (x - hi.astype(F32) - lo.astype(F32)).astype(BF16)
        return hi, lo, lo2

    beta_all, gam_all, gam_rows, egam_all, ekd_all = [], [], [], [], []
    for j in range(cpb):
        rs = slice(j * CHUNK, (j + 1) * CHUNK)
        valid = valid_ref[rs, :]
        small = small_ref[rs, :]
        beta_all.append(jax.nn.sigmoid(small) * valid)
        g_all = -jnp.exp(alog_ref[...]) * jax.nn.softplus(small + dtb_ref[...]) * valid
        gam = _dot(jnp.concatenate([tri_l] * 3, axis=1), jnp.concatenate(split3(g_all), axis=0))
        g_rows = g_all.T[DN_HEADS:2 * DN_HEADS, :]
        gam_rows.append(_dot(jnp.concatenate(split3(g_rows), axis=1), jnp.concatenate([tri_u] * 3, axis=0)))
        gam_all.append(gam)
        egam_all.append(jnp.exp(gam))
        ekd_all.append(jnp.exp(gam[CHUNK - 1:CHUNK, :] - gam))

    pairs = range(DN_HEADS // 2)
    units = [(j, p) for j in range(cpb) for p in pairs]
    lane_c = lax.broadcasted_iota(jnp.int32, (CHUNK, 2 * CHUNK), 1)
    prow = lax.broadcasted_iota(jnp.int32, (CHUNK, 2 * CHUNK), 0)
    pcol = lane_c & (CHUNK - 1)
    low_c = lane_c < CHUNK
    low_k = lax.broadcasted_iota(jnp.int32, (CHUNK, 2 * DN_DK), 1) < DN_DK
    p_incl = prow >= pcol
    p_strict = prow > pcol

    def head_cols(x_all, off, p, low):
        a = x_all[:, off + 2 * p:off + 2 * p + 1]
        b = x_all[:, off + 2 * p + 1:off + 2 * p + 2]
        return jnp.where(low, a, b)

    slab = lambda j, lo, width: qkv_ref[j * CHUNK:(j + 1) * CHUNK, lo:lo + width]
    qb = [slab(j, p * 2 * DN_DK, 2 * DN_DK) for j, p in units]
    kb = [slab(j, DN_QK + p * 2 * DN_DK, 2 * DN_DK) for j, p in units]
    q = [x.astype(F32) for x in qb]
    k = [x.astype(F32) for x in kb]
    v = [slab(j, 2 * DN_QK + p * 2 * DN_DV, 2 * DN_DV).astype(F32) for j, p in units]
    beta_c = [head_cols(beta_all[j], 0, p, low_c) for j, p in units]
    gcol = [head_cols(gam_all[j], DN_HEADS, p, low_c) for j, p in units]
    grow = [jnp.concatenate([gam_rows[j][2 * p:2 * p + 1, :], gam_rows[j][2 * p + 1:2 * p + 2, :]], axis=1)
            for j, p in units]
    decay = [jnp.exp(jnp.where(p_incl, gc - gr, -jnp.inf)) for gc, gr in zip(gcol, grow)]
    zk = jnp.zeros((CHUNK, 2 * DN_DK), BF16)
    kq = [_dot_nt(jnp.concatenate([kbp, qbp], axis=0),
                  jnp.concatenate([jnp.where(low_k, kbp, zk), jnp.where(low_k, zk, kbp)], axis=0))
          for kbp, qbp in zip(kb, qb)]
    a_mat = [jnp.where(p_strict, b * x[:CHUNK] * d, 0.0) for b, x, d in zip(beta_c, kq, decay)]
    t = _unit_lower_inverses(a_mat, prow, pcol, low_c)

    zr = jnp.zeros((CHUNK, DN_DV + DN_DK), F32)
    uw = []
    for n, (j, p) in enumerate(units):
        r = []
        for i in range(2):
            h = 2 * p + i
            b = beta_all[j][:, h:h + 1]
            e = egam_all[j][:, DN_HEADS + h:DN_HEADS + h + 1]
            r.append(jnp.concatenate([v[n][:, i * DN_DV:(i + 1) * DN_DV] * b,
                                      k[n][:, i * DN_DK:(i + 1) * DN_DK] * (b * e)], axis=1))
        rhs = jnp.concatenate([jnp.concatenate([r[0], zr], axis=1),
                               jnp.concatenate([zr, r[1]], axis=1)], axis=0).astype(BF16)
        th, tl = t[n]
        uw.append(_dot(jnp.concatenate([th, tl], axis=1), jnp.concatenate([rhs, rhs], axis=0)))

    dvk = DN_DV + DN_DK
    n_pairs = len(pairs)
    rows = lambda per_chunk: jnp.concatenate([per_chunk(j) for j in range(cpb)], axis=0)
    lanes = lambda parts: jnp.concatenate(parts, axis=1)
    u_ref[...] = rows(lambda j: lanes([uw[j * n_pairs + p][:, i * dvk:i * dvk + DN_DV]
                                       for p in pairs for i in range(2)]))
    w_ref[...] = rows(lambda j: lanes([uw[j * n_pairs + p][:, i * dvk + DN_DV:(i + 1) * dvk]
                                       for p in pairs for i in range(2)])).astype(BF16)
    qg_ref[...] = rows(lambda j: lanes([q[j * n_pairs + p] * head_cols(egam_all[j], DN_HEADS, p, low_k)
                                        for p in pairs])).astype(BF16)
    kd_ref[...] = rows(lambda j: lanes([k[j * n_pairs + p] * head_cols(ekd_all[j], DN_HEADS, p, low_k)
                                        for p in pairs])).astype(BF16)
    attn_ref[...] = rows(lambda j: lanes([kq[j * n_pairs + p][CHUNK:] * decay[j * n_pairs + p]
                                          for p in pairs])).astype(BF16)
    eg_ref[...] = rows(lambda j: jnp.broadcast_to(jnp.exp(gam_rows[j][:, CHUNK - 1:CHUNK]),
                                                  (DN_HEADS, DN_DV)))

    @pl.when(c == pl.num_programs(1) - 1)
    def _():
        sfin_ref[...] = state[...]


def deltanet_mixer(proj, small, valid, a_log, dt_bias, s0):
    b, s, _ = proj.shape
    nc = s // CHUNK
    cpb = DN_CHUNKS_PER_STEP if nc % DN_CHUNKS_PER_STEP == 0 else 1
    rows = cpb * CHUNK
    steps = nc // cpb
    valid_tab = jnp.broadcast_to(valid[:, None], (s, LANES))
    pad_heads = lambda v: jnp.zeros((1, LANES), F32).at[0, DN_HEADS:2 * DN_HEADS].set(v)
    cur = lambda c: jnp.minimum(c, steps - 1)
    prev = lambda c: jnp.maximum(c - 1, 0)
    const = lambda *shape: pl.BlockSpec(shape, lambda bi, c: (0,) * len(shape))
    in_specs = [pl.BlockSpec((None, rows, DN_CONV_CH), lambda bi, c: (bi, cur(c), 0)),
                pl.BlockSpec((None, rows, LANES), lambda bi, c: (bi, cur(c), 0)),
                pl.BlockSpec((rows, LANES), lambda bi, c: (cur(c), 0)),
                const(1, LANES), const(1, LANES), const(DN_HEADS, DN_DK, DN_DV)]
    out_specs = [pl.BlockSpec((None, rows, DN_V), lambda bi, c: (bi, prev(c), 0)),
                 pl.BlockSpec((None, DN_HEADS, DN_DK, DN_DV), lambda bi, c: (bi, 0, 0, 0))]
    out_shape = [jax.ShapeDtypeStruct((b, s, DN_V), BF16),
                 jax.ShapeDtypeStruct((b, DN_HEADS, DN_DK, DN_DV), F32)]
    scratch = [pltpu.VMEM((rows, DN_V), F32),
               pltpu.VMEM((rows, DN_QK), BF16),
               pltpu.VMEM((rows, DN_QK), BF16),
               pltpu.VMEM((rows, DN_QK), BF16),
               pltpu.VMEM((rows, DN_HEADS * CHUNK), BF16),
               pltpu.VMEM((cpb * DN_HEADS, DN_DV), F32),
               pltpu.VMEM((DN_HEADS, DN_DK, DN_DV), F32)]
    return pl.pallas_call(
        functools.partial(_deltanet_kernel, cpb=cpb),
        grid=(b, steps + 1), in_specs=in_specs, out_specs=out_specs, out_shape=out_shape,
        scratch_shapes=scratch,
        compiler_params=_cparams(("parallel", "arbitrary")), name="deltanet")(
            proj, small, valid_tab, pad_heads(a_log), pad_heads(dt_bias), s0)


def _layer_retention(h, pos, valid, chunk, s0, p):
    b, s, d = h.shape
    h2 = h.reshape(b * s, d)
    cos, sin = _rope_tables(pos)
    valid_tab = jnp.broadcast_to(valid[:, None], (s, LANES))
    proj2 = rms_inproj(h2, p["mix_norm_w"], p["w_in"], rope=(cos, sin, valid_tab))
    o, s_fin = retention(proj2.reshape(b, s, -1), s0, chunk)
    gate_blk = (2 * RET_QK + RET_V) // RET_V
    h2 = out_ffn(h2, o.reshape(b * s, RET_V), proj2, gate_blk, p["gn_w"], p["w_out"], p["ffn_norm_w"],
                 p["w_gate"], p["w_up"], p["w_down"])
    return h2.reshape(b, s, d), s_fin


def _layer_deltanet(h, valid, s0, halo0, p, final_w=None, run_tail=True):
    b, s, d = h.shape
    h2 = h.reshape(b * s, d)
    valid_tab = jnp.broadcast_to(valid[:, None], (s, LANES))
    proj2, small, tail = dn_inproj(h2, p["mix_norm_w"], p["w_in"], p["w_in_small"], p["conv_w"], halo0,
                                   valid_tab)
    o, s_fin = deltanet_mixer(proj2.reshape(b, s, -1), small.reshape(b, s, -1), valid, p["a_log"],
                              p["dt_bias"], s0)
    out = None
    if run_tail:
        gate_blk = DN_CONV_CH // DN_V
        out = out_ffn(h2, o.reshape(b * s, DN_V), proj2, gate_blk, p["norm_w"], p["w_out"],
                      p["ffn_norm_w"], p["w_gate"], p["w_up"], p["w_down"], final_w).reshape(b, s, d)
    return out, s_fin, tail


def kernel(x, meta_tokens, mix_norm_w, ffn_norm_w, ret_w_in, ret_gn_w, ret_w_out, dn_w_in, dn_conv_w,
           dn_a_log, dn_dt_bias, dn_norm_w, dn_w_out, ffn_w_gate, ffn_w_up, ffn_w_down, final_norm_w):
    b, s, d = x.shape
    bf = lambda t: t.astype(BF16)
    dn_main = DN_CONV_CH + DN_V
    small_w = jnp.zeros((d, LANES), F32).at[:, :2 * DN_HEADS].set(dn_w_in[0][:, dn_main:])
    p_ret = dict(mix_norm_w=mix_norm_w[0], w_in=bf(ret_w_in[0]), gn_w=ret_gn_w[0], w_out=bf(ret_w_out[0]),
                 ffn_norm_w=ffn_norm_w[0], w_gate=bf(ffn_w_gate[0]), w_up=bf(ffn_w_up[0]),
                 w_down=bf(ffn_w_down[0]))
    p_dn = dict(mix_norm_w=mix_norm_w[1], w_in=bf(dn_w_in[0][:, :dn_main]), w_in_small=bf(small_w),
                conv_w=dn_conv_w[0], a_log=dn_a_log[0], dt_bias=dn_dt_bias[0], norm_w=dn_norm_w[0],
                w_out=bf(dn_w_out[0]), ffn_norm_w=ffn_norm_w[1], w_gate=bf(ffn_w_gate[1]),
                w_up=bf(ffn_w_up[1]), w_down=bf(ffn_w_down[1]))

    h_meta = jnp.concatenate([jnp.zeros((PAD, d), x.dtype), meta_tokens.astype(x.dtype)], axis=0)[None]
    pos_meta = (jnp.arange(CHUNK) - PAD).astype(F32)
    valid_meta = (pos_meta >= 0).astype(F32)
    ret_s0 = jnp.zeros((RET_HEADS, RET_DK, RET_DV), F32)
    dn_s0 = jnp.zeros((DN_HEADS, DN_DK, DN_DV), F32)
    halo_zero = jnp.zeros((SUBLANES, DN_CONV_CH), F32)
    h_meta, ret_state = _layer_retention(h_meta, pos_meta, valid_meta, CHUNK, ret_s0, p_ret)
    _, dn_state, halo_meta = _layer_deltanet(h_meta, valid_meta, dn_s0, halo_zero, p_dn, run_tail=False)

    pos = (jnp.arange(s) + N_META).astype(F32)
    valid = jnp.ones((s,), F32)
    h, _ = _layer_retention(x, pos, valid, RET_CHUNK, ret_state[0], p_ret)
    out, _, _ = _layer_deltanet(h, valid, dn_state[0], halo_meta, p_dn, final_w=final_norm_w)
    return out
```

```python
import functools

import jax
import jax.numpy as jnp
from jax import lax
from jax.experimental import pallas as pl
from jax.experimental.pallas import tpu as pltpu

F32 = jnp.float32
BF16 = jnp.bfloat16

N_META = 16
CHUNK = 64
PAD = CHUNK - N_META
RMS_EPS = 1e-6
RET_HEADS = 4
RET_DK = 256
RET_DV = 512
RET_QK = RET_HEADS * RET_DK
RET_V = RET_HEADS * RET_DV
ROPE_BASE = 10000.0
DN_HEADS = 8
DN_DK = 128
DN_DV = 256
DN_QK = DN_HEADS * DN_DK
DN_V = DN_HEADS * DN_DV
DN_CONV_CH = 2 * DN_QK + DN_V
CONV_K = 4

LANES = 128
SUBLANES = 8
VMEM_LIMIT_BYTES = 56 * 1024 * 1024
TOKEN_BLOCK = 512
PROJ_N_CHUNK = 1024
FFN_ROW_SPLITS = 2
RET_CHUNK = 256
DN_CHUNKS_PER_STEP = 4


def _cparams(sem):
    return pltpu.CompilerParams(dimension_semantics=sem, vmem_limit_bytes=VMEM_LIMIT_BYTES)


def _resident(shape):
    nd = len(shape)
    return pl.BlockSpec(shape, lambda *_: (0,) * nd, pipeline_mode=pl.Buffered(1))


def _rms(x, w):
    ms = jnp.mean(x * x, axis=-1, keepdims=True)
    return x * lax.rsqrt(ms + RMS_EPS) * w


def _silu(x):
    h = 0.5 * x
    return h + h * jnp.tanh(h)


def _dot(a, b):
    return jnp.dot(a, b, preferred_element_type=F32)


def _dot_nt(a, b):
    return lax.dot_general(a, b, (((1,), (1,)), ((), ())), preferred_element_type=F32)


def _dot_tn(a, b):
    return lax.dot_general(a, b, (((0,), (0,)), ((), ())), preferred_element_type=F32)


def _rms_inproj_kernel(*refs, has_rope):
    if has_rope:
        h_ref, nw_ref, w_ref, cos_ref, sin_ref, valid_ref, o_ref = refs
    else:
        h_ref, nw_ref, w_ref, o_ref = refs
    xn = _rms(h_ref[...], nw_ref[...]).astype(BF16)
    n_total = o_ref.shape[-1]
    for n0 in range(0, n_total, PROJ_N_CHUNK):
        n1 = min(n0 + PROJ_N_CHUNK, n_total)
        res = _dot(xn, w_ref[:, n0:n1])
        if has_rope and n0 < 2 * RET_QK:
            cos, sin = cos_ref[...], sin_ref[...]
            half = RET_DK // 2
            parts = []
            for c0 in range(0, n1 - n0, RET_DK):
                t1, t2 = res[:, c0:c0 + half], res[:, c0 + half:c0 + RET_DK]
                r1, r2 = t1 * cos - t2 * sin, t1 * sin + t2 * cos
                if n0 + c0 >= RET_QK:
                    valid = valid_ref[...]
                    r1 = r1 * (RET_DK ** -0.5) * valid
                    r2 = r2 * (RET_DK ** -0.5) * valid
                parts += [r1, r2]
            res = jnp.concatenate(parts, axis=1)
        o_ref[:, n0:n1] = res.astype(o_ref.dtype)


def rms_inproj(h2d, norm_w, w, rope=None):
    m, d = h2d.shape
    n = w.shape[1]
    bm = min(TOKEN_BLOCK, m)
    has_rope = rope is not None
    in_specs = [pl.BlockSpec((bm, d), lambda i: (i, 0)), _resident((1, d)), _resident((d, n))]
    args = [h2d, norm_w.reshape(1, d), w]
    if has_rope:
        assert PROJ_N_CHUNK % RET_DK == 0 and RET_QK % PROJ_N_CHUNK == 0
        blocks_per_seq = rope[0].shape[0] // bm
        in_specs += [pl.BlockSpec((bm, LANES), lambda i: (i % blocks_per_seq, 0))] * 3
        args += list(rope)
    return pl.pallas_call(
        functools.partial(_rms_inproj_kernel, has_rope=has_rope),
        grid=(m // bm,), in_specs=in_specs, out_specs=pl.BlockSpec((bm, n), lambda i: (i, 0)),
        out_shape=jax.ShapeDtypeStruct((m, n), BF16),
        compiler_params=_cparams(("parallel",)), name="rms_inproj")(*args)


def _dn_inproj_kernel(h_ref, nw_ref, w_ref, ws_ref, convw_ref, halo0_ref, valid_ref,
                      o_ref, os_ref, tail_ref, carry, *, blocks_per_seq):
    i = pl.program_id(0)

    @pl.when(i % blocks_per_seq == 0)
    def _():
        carry[...] = halo0_ref[...]

    xn = _rms(h_ref[...], nw_ref[...]).astype(BF16)
    bm = h_ref.shape[0]
    valid = valid_ref[...]
    n_total = o_ref.shape[-1]
    chunks = [(n0, min(n0 + PROJ_N_CHUNK, n_total)) for n0 in range(0, n_total, PROJ_N_CHUNK)]
    res_next = _dot(xn, w_ref[:, chunks[0][0]:chunks[0][1]])
    outs = []
    for ci, (n0, n1) in enumerate(chunks):
        res = res_next
        if ci + 1 < len(chunks):
            res_next = _dot(xn, w_ref[:, chunks[ci + 1][0]:chunks[ci + 1][1]])
        if n0 < DN_CONV_CH:
            x = res * jnp.concatenate([valid] * ((n1 - n0) // LANES), axis=1)
            ext = jnp.concatenate([carry[:, n0:n1], x], axis=0)
            acc = x * convw_ref[CONV_K - 1:CONV_K, n0:n1]
            for j in range(CONV_K - 1):
                delayed = pltpu.roll(ext, shift=CONV_K - 1 - j, axis=0)[SUBLANES:]
                acc = acc + delayed * convw_ref[j:j + 1, n0:n1]
            carry[:, n0:n1] = x[bm - SUBLANES:, :]
            res = _silu(acc)
            if n0 < 2 * DN_QK:
                parts = []
                for c0 in range(0, n1 - n0, DN_DK):
                    y = res[:, c0:c0 + DN_DK]
                    y = y * lax.rsqrt(jnp.sum(y * y, axis=-1, keepdims=True) + RMS_EPS)
                    parts.append(y * (DN_DK ** -0.5) if n0 < DN_QK else y)
                res = jnp.concatenate(parts, axis=1)
        outs.append(res.astype(o_ref.dtype))
    o_ref[...] = jnp.concatenate(outs, axis=1)
    os_ref[...] = _dot(xn, ws_ref[...])

    @pl.when(i == pl.num_programs(0) - 1)
    def _():
        tail_ref[...] = carry[...]


def dn_inproj(h2d, norm_w, w, w_small, conv_w, halo0, valid_tab):
    m, d = h2d.shape
    n = w.shape[1]
    ns = w_small.shape[1]
    bm = min(TOKEN_BLOCK, m)
    assert DN_QK % PROJ_N_CHUNK == 0 and DN_CONV_CH % PROJ_N_CHUNK == 0 and PROJ_N_CHUNK % DN_DK == 0
    blocks_per_seq = valid_tab.shape[0] // bm
    row = lambda width: pl.BlockSpec((bm, width), lambda i: (i, 0))
    in_specs = [row(d), _resident((1, d)), _resident((d, n)), _resident((d, ns)),
                _resident((CONV_K, DN_CONV_CH)), _resident((SUBLANES, DN_CONV_CH)),
                pl.BlockSpec((bm, LANES), lambda i: (i % blocks_per_seq, 0))]
    out_specs = [row(n), row(ns), pl.BlockSpec((SUBLANES, DN_CONV_CH), lambda i: (0, 0))]
    out_shape = [jax.ShapeDtypeStruct((m, n), BF16), jax.ShapeDtypeStruct((m, ns), F32),
                 jax.ShapeDtypeStruct((SUBLANES, DN_CONV_CH), F32)]
    return pl.pallas_call(
        functools.partial(_dn_inproj_kernel, blocks_per_seq=blocks_per_seq),
        grid=(m // bm,), in_specs=in_specs, out_specs=out_specs, out_shape=out_shape,
        scratch_shapes=[pltpu.VMEM((SUBLANES, DN_CONV_CH), F32)],
        compiler_params=_cparams(("arbitrary",)), name="dn_inproj")(
            h2d, norm_w.reshape(1, d), w, w_small, conv_w, halo0, valid_tab)


def _out_ffn_kernel(*refs, final, head_dv):
    if final:
        h_ref, o_ref, g_ref, gnw_ref, wo_ref, nw_ref, wg_ref, wu_ref, wd_ref, fw_ref, out_ref = refs
    else:
        h_ref, o_ref, g_ref, gnw_ref, wo_ref, nw_ref, wg_ref, wu_ref, wd_ref, out_ref = refs
    gnw = gnw_ref[...]
    bm = h_ref.shape[0]
    n_sub = FFN_ROW_SPLITS if bm % (FFN_ROW_SPLITS * SUBLANES * 2) == 0 else 1
    outs = []
    for r0 in range(0, bm, bm // n_sub):
        rs = slice(r0, r0 + bm // n_sub)
        h1 = h_ref[rs, :]
        for c0 in range(0, o_ref.shape[1], head_dv):
            x = o_ref[rs, c0:c0 + head_dv].astype(F32)
            xn = x * lax.rsqrt(jnp.mean(x * x, axis=-1, keepdims=True) + RMS_EPS) * gnw
            og = (xn * _silu(g_ref[rs, c0:c0 + head_dv].astype(F32))).astype(BF16)
            h1 = h1 + _dot(og, wo_ref[c0:c0 + head_dv, :])
        xn = _rms(h1, nw_ref[...]).astype(BF16)
        act = (_silu(_dot(xn, wg_ref[...])) * _dot(xn, wu_ref[...])).astype(BF16)
        h2 = h1 + _dot(act, wd_ref[...])
        if final:
            h2 = _rms(h2, fw_ref[...])
        outs.append(h2)
    out_ref[...] = jnp.concatenate(outs, axis=0)


def out_ffn(h2d, o2d, proj2d, gate_blk, gn_w, w_out, norm_w, w_gate, w_up, w_down, final_w=None):
    m, d = h2d.shape
    dv = o2d.shape[1]
    head_dv = gn_w.shape[0]
    f = w_gate.shape[1]
    bm = min(TOKEN_BLOCK, m)
    final = final_w is not None
    row = lambda width, blk=0: pl.BlockSpec((bm, width), lambda i: (i, blk))
    in_specs = [row(d), row(dv), row(dv, gate_blk), _resident((1, head_dv)), _resident((dv, d)),
                _resident((1, d)), _resident((d, f)), _resident((d, f)), _resident((f, d))]
    args = [h2d, o2d, proj2d, gn_w.reshape(1, head_dv), w_out, norm_w.reshape(1, d), w_gate, w_up, w_down]
    if final:
        in_specs.append(_resident((1, d)))
        args.append(final_w.reshape(1, d))
    return pl.pallas_call(
        functools.partial(_out_ffn_kernel, final=final, head_dv=head_dv),
        grid=(m // bm,), in_specs=in_specs, out_specs=row(d),
        out_shape=jax.ShapeDtypeStruct((m, d), F32),
        compiler_params=_cparams(("parallel",)), name="out_ffn")(*args)


def _retention_kernel(qk_ref, v_ref, dmask_ref, xi_ref, zeta_ref, gc_ref, s0_ref, o_ref, sfin_ref,
                      state):
    c = pl.program_id(1)

    @pl.when(c == 0)
    def _():
        state[...] = s0_ref[...]

    heads = range(RET_HEADS)
    wide = lambda tab: jnp.concatenate([tab] * (RET_DK // LANES), axis=1)
    qb = [qk_ref[:, h * RET_DK:(h + 1) * RET_DK] for h in heads]
    kb = [qk_ref[:, RET_QK + h * RET_DK:RET_QK + (h + 1) * RET_DK] for h in heads]
    qx = [(qb[h].astype(F32) * wide(xi_ref[h])).astype(BF16) for h in heads]
    kz = [(kb[h].astype(F32) * wide(zeta_ref[h])).astype(BF16) for h in heads]
    vb = [v_ref[:, h * RET_DV:(h + 1) * RET_DV] for h in heads]
    s_prev = [state[h] for h in heads]
    scores = [(_dot_nt(qb[h], kb[h]) * dmask_ref[h]).astype(BF16) for h in heads]
    o = [_dot(scores[h], vb[h]) + _dot(qx[h], s_prev[h].astype(BF16)) for h in heads]
    s_new = [gc_ref[h, 0:1, :] * s_prev[h] + _dot_tn(kz[h], vb[h]) for h in heads]
    state[...] = jnp.stack(s_new, axis=0)
    o_ref[...] = jnp.concatenate(o, axis=1).astype(o_ref.dtype)

    @pl.when(c == pl.num_programs(1) - 1)
    def _():
        sfin_ref[...] = state[...]


def _retention_tables(chunk):
    log_gamma = jnp.log1p(-jnp.exp2(-5.0 - jnp.arange(RET_HEADS, dtype=F32)))
    idx = jnp.arange(chunk, dtype=F32)
    rel = idx[:, None] - idx[None, :]
    dmask = jnp.where((rel >= 0)[None],
                      jnp.exp(log_gamma[:, None, None] * jnp.maximum(rel, 0.0)), 0.0)
    xi = jnp.exp(log_gamma[:, None] * (idx[None, :] + 1.0))
    zeta = jnp.exp(log_gamma[:, None] * (chunk - 1.0 - idx[None, :]))
    gamma_c = jnp.exp(log_gamma * chunk)
    xi = jnp.broadcast_to(xi[:, :, None], (RET_HEADS, chunk, LANES))
    zeta = jnp.broadcast_to(zeta[:, :, None], (RET_HEADS, chunk, LANES))
    gc = jnp.broadcast_to(gamma_c[:, None, None], (RET_HEADS, SUBLANES, RET_DV))
    return dmask, xi, zeta, gc


def _rope_tables(pos):
    half = RET_DK // 2
    inv_freq = ROPE_BASE ** (-jnp.arange(half, dtype=F32) / half)
    ang = pos[:, None] * inv_freq[None, :]
    return jnp.cos(ang), jnp.sin(ang)


def retention(proj, s0, chunk):
    b, s, _ = proj.shape
    nc = s // chunk
    dmask, xi, zeta, gc = _retention_tables(chunk)
    assert 2 * RET_QK == RET_V
    tok = lambda blk: pl.BlockSpec((None, chunk, RET_V), lambda bi, c: (bi, c, blk))
    whole = lambda *shape: pl.BlockSpec(shape, lambda bi, c: (0,) * len(shape))
    in_specs = [tok(0), tok(1),
                whole(RET_HEADS, chunk, chunk), whole(RET_HEADS, chunk, LANES),
                whole(RET_HEADS, chunk, LANES), whole(RET_HEADS, SUBLANES, RET_DV),
                whole(RET_HEADS, RET_DK, RET_DV)]
    out_specs = [tok(0),
                 pl.BlockSpec((None, RET_HEADS, RET_DK, RET_DV), lambda bi, c: (bi, 0, 0, 0))]
    out_shape = [jax.ShapeDtypeStruct((b, s, RET_V), BF16),
                 jax.ShapeDtypeStruct((b, RET_HEADS, RET_DK, RET_DV), F32)]
    return pl.pallas_call(
        _retention_kernel, grid=(b, nc), in_specs=in_specs, out_specs=out_specs,
        out_shape=out_shape, scratch_shapes=[pltpu.VMEM((RET_HEADS, RET_DK, RET_DV), F32)],
        compiler_params=_cparams(("parallel", "arbitrary")), name="retention")(
            proj, proj, dmask, xi, zeta, gc, s0)


def _split(x):
    hi = x.astype(BF16)
    lo = (x - hi.astype(F32)).astype(BF16)
    return hi, lo


def _pair_block_diag(y, low):
    zero = jnp.zeros_like(y)
    return jnp.concatenate([jnp.where(low, y, zero), jnp.where(low, zero, y)], axis=0)


def _mm_pair(xs, ys, low):
    (xh, xl), (yh, yl) = xs, ys
    lhs = jnp.concatenate([xh, xh, xl], axis=1)
    rhs = jnp.concatenate([_pair_block_diag(yh, low), _pair_block_diag(yl, low),
                           _pair_block_diag(yh, low)], axis=0)
    return _dot(lhs, rhs)


def _unit_lower_inverses(a_list, row, col, low):
    eye = (row == col).astype(F32)
    blk = lambda n: (row // n) == (col // n)
    mm = functools.partial(_mm_pair, low=low)
    sq = lambda xs: [mm(x, x) for x in xs]
    n1 = [jnp.where(blk(16), -a, 0.0) for a in a_list]
    n2 = sq([_split(x) for x in n1])
    n4 = sq([_split(x) for x in n2])
    n8 = sq([_split(x) for x in n4])
    t = [mm(_split(eye + a), _split(eye + b)) for a, b in zip(n1, n2)]
    t = [mm(_split(x), _split(eye + y)) for x, y in zip(t, n4)]
    t = [mm(_split(x), _split(eye + y)) for x, y in zip(t, n8)]
    mm1 = lambda xh, yh: _dot(xh, _pair_block_diag(yh, low))
    for n in (32, 64):
        off_mask = blk(n) & ~blk(n // 2)
        th = [x.astype(BF16) for x in t]
        to = [mm1(x, jnp.where(off_mask, a, 0.0).astype(BF16)) for x, a in zip(th, a_list)]
        t = [x - mm1(y.astype(BF16), xh) for x, y, xh in zip(t, to, th)]
    return [_split(x) for x in t]


def _dn_recurrence(u_s, w_s, qg_s, kd_s, attn_s, eg_s, state, cpb):
    heads = range(DN_HEADS)
    dk = [slice(h * DN_DK, (h + 1) * DN_DK) for h in heads]
    dv = [slice(h * DN_DV, (h + 1) * DN_DV) for h in heads]
    s_cur = [state[h] for h in heads]
    out_rows = []
    for j in range(cpb):
        rs = slice(j * CHUNK, (j + 1) * CHUNK)
        wq = [jnp.concatenate([w_s[rs, dk[h]], qg_s[rs, dk[h]]], axis=0) for h in heads]
        ws = [_dot(wq[h], s_cur[h].astype(BF16)) for h in heads]
        vb = [(u_s[rs, dv[h]] - ws[h][:CHUNK]).astype(BF16) for h in heads]
        o = [ws[h][CHUNK:] + _dot(attn_s[rs, h * CHUNK:(h + 1) * CHUNK], vb[h]) for h in heads]
        s_cur = [s_cur[h] * eg_s[j * DN_HEADS + h:j * DN_HEADS + h + 1, :] + _dot_tn(kd_s[rs, dk[h]], vb[h])
                 for h in heads]
        out_rows.append(jnp.concatenate(o, axis=1))
    state[...] = jnp.stack(s_cur, axis=0)
    return jnp.concatenate(out_rows, axis=0)


def _deltanet_kernel(qkv_ref, small_ref, valid_ref, alog_ref, dtb_ref, s0_ref, o_ref, sfin_ref,
                     u_ref, w_ref, qg_ref, kd_ref, attn_ref, eg_ref, state, *, cpb):
    c = pl.program_id(1)

    @pl.when(c == 0)
    def _():
        for ref in (u_ref, w_ref, qg_ref, kd_ref, attn_ref, eg_ref):
            ref[...] = jnp.zeros(ref.shape, ref.dtype)

    @pl.when(c <= 1)
    def _():
        state[...] = s0_ref[...]

    o_ref[...] = _dn_recurrence(u_ref, w_ref, qg_ref, kd_ref, attn_ref, eg_ref, state,
                                cpb).astype(o_ref.dtype)

    row = lax.broadcasted_iota(jnp.int32, (CHUNK, CHUNK), 0)
    col = lax.broadcasted_iota(jnp.int32, (CHUNK, CHUNK), 1)
    tri_l = (row >= col).astype(BF16)
    tri_u = (row <= col).astype(BF16)

    def split3(x):
        hi, lo = _split(x)
        lo2 = (x - hi.astype(F32) - lo.astype(F32)).astype(BF16)
        return hi, lo, lo2

    beta_all, gam_all, gam_rows, egam_all, ekd_all = [], [], [], [], []
    for j in range(cpb):
        rs = slice(j * CHUNK, (j + 1) * CHUNK)
        valid = valid_ref[rs, :]
        small = small_ref[rs, :]
        beta_all.append(jax.nn.sigmoid(small) * valid)
        g_all = -jnp.exp(alog_ref[...]) * jax.nn.softplus(small + dtb_ref[...]) * valid
        gam = _dot(jnp.concatenate([tri_l] * 3, axis=1), jnp.concatenate(split3(g_all), axis=0))
        g_rows = g_all.T[DN_HEADS:2 * DN_HEADS, :]
        gam_rows.append(_dot(jnp.concatenate(split3(g_rows), axis=1), jnp.concatenate([tri_u] * 3, axis=0)))
        gam_all.append(gam)
        egam_all.append(jnp.exp(gam))
        ekd_all.append(jnp.exp(gam[CHUNK - 1:CHUNK, :] - gam))

    pairs = range(DN_HEADS // 2)
    units = [(j, p) for j in range(cpb) for p in pairs]
    lane_c = lax.broadcasted_iota(jnp.int32, (CHUNK, 2 * CHUNK), 1)
    prow = lax.broadcasted_iota(jnp.int32, (CHUNK, 2 * CHUNK), 0)
    pcol = lane_c & (CHUNK - 1)
    low_c = lane_c < CHUNK
    low_k = lax.broadcasted_iota(jnp.int32, (CHUNK, 2 * DN_DK), 1) < DN_DK
    p_incl = prow >= pcol
    p_strict = prow > pcol

    def head_cols(x_all, off, p, low):
        a = x_all[:, off + 2 * p:off + 2 * p + 1]
        b = x_all[:, off + 2 * p + 1:off + 2 * p + 2]
        return jnp.where(low, a, b)

    slab = lambda j, lo, width: qkv_ref[j * CHUNK:(j + 1) * CHUNK, lo:lo + width]
    qb = [slab(j, p * 2 * DN_DK, 2 * DN_DK) for j, p in units]
    kb = [slab(j, DN_QK + p * 2 * DN_DK, 2 * DN_DK) for j, p in units]
    q = [x.astype(F32) for x in qb]
    k = [x.astype(F32) for x in kb]
    v = [slab(j, 2 * DN_QK + p * 2 * DN_DV, 2 * DN_DV).astype(F32) for j, p in units]
    beta_c = [head_cols(beta_all[j], 0, p, low_c) for j, p in units]
    gcol = [head_cols(gam_all[j], DN_HEADS, p, low_c) for j, p in units]
    grow = [jnp.concatenate([gam_rows[j][2 * p:2 * p + 1, :], gam_rows[j][2 * p + 1:2 * p + 2, :]], axis=1)
            for j, p in units]
    decay = [jnp.exp(jnp.where(p_incl, gc - gr, -jnp.inf)) for gc, gr in zip(gcol, grow)]
    zk = jnp.zeros((CHUNK, 2 * DN_DK), BF16)
    kq = [_dot_nt(jnp.concatenate([kbp, qbp], axis=0),
                  jnp.concatenate([jnp.where(low_k, kbp, zk), jnp.where(low_k, zk, kbp)], axis=0))
          for kbp, qbp in zip(kb, qb)]
    a_mat = [jnp.where(p_strict, b * x[:CHUNK] * d, 0.0) for b, x, d in zip(beta_c, kq, decay)]
    t = _unit_lower_inverses(a_mat, prow, pcol, low_c)

    zr = jnp.zeros((CHUNK, DN_DV + DN_DK), F32)
    uw = []
    for n, (j, p) in enumerate(units):
        r = []
        for i in range(2):
            h = 2 * p + i
            b = beta_all[j][:, h:h + 1]
            e = egam_all[j][:, DN_HEADS + h:DN_HEADS + h + 1]
            r.append(jnp.concatenate([v[n][:, i * DN_DV:(i + 1) * DN_DV] * b,
                                      k[n][:, i * DN_DK:(i + 1) * DN_DK] * (b * e)], axis=1))
        rhs = jnp.concatenate([jnp.concatenate([r[0], zr], axis=1),
                               jnp.concatenate([zr, r[1]], axis=1)], axis=0).astype(BF16)
        th, tl = t[n]
        uw.append(_dot(jnp.concatenate([th, tl], axis=1), jnp.concatenate([rhs, rhs], axis=0)))

    dvk = DN_DV + DN_DK
    n_pairs = len(pairs)
    rows = lambda per_chunk: jnp.concatenate([per_chunk(j) for j in range(cpb)], axis=0)
    lanes = lambda parts: jnp.concatenate(parts, axis=1)
    u_ref[...] = rows(lambda j: lanes([uw[j * n_pairs + p][:, i * dvk:i * dvk + DN_DV]
                                       for p in pairs for i in range(2)]))
    w_ref[...] = rows(lambda j: lanes([uw[j * n_pairs + p][:, i * dvk + DN_DV:(i + 1) * dvk]
                                       for p in pairs for i in range(2)])).astype(BF16)
    qg_ref[...] = rows(lambda j: lanes([q[j * n_pairs + p] * head_cols(egam_all[j], DN_HEADS, p, low_k)
                                        for p in pairs])).astype(BF16)
    kd_ref[...] = rows(lambda j: lanes([k[j * n_pairs + p] * head_cols(ekd_all[j], DN_HEADS, p, low_k)
                                        for p in pairs])).astype(BF16)
    attn_ref[...] = rows(lambda j: lanes([kq[j * n_pairs + p][CHUNK:] * decay[j * n_pairs + p]
                                          for p in pairs])).astype(BF16)
    eg_ref[...] = rows(lambda j: jnp.broadcast_to(jnp.exp(gam_rows[j][:, CHUNK - 1:CHUNK]),
                                                  (DN_HEADS, DN_DV)))

    @pl.when(c == pl.num_programs(1) - 1)
    def _():
        sfin_ref[...] = state[...]


def deltanet_mixer(proj, small, valid, a_log, dt_bias, s0):
    b, s, _ = proj.shape
    nc = s // CHUNK
    cpb = DN_CHUNKS_PER_STEP if nc % DN_CHUNKS_PER_STEP == 0 else 1
    rows = cpb * CHUNK
    steps = nc // cpb
    valid_tab = jnp.broadcast_to(valid[:, None], (s, LANES))
    pad_heads = lambda v: jnp.zeros((1, LANES), F32).at[0, DN_HEADS:2 * DN_HEADS].set(v)
    cur = lambda c: jnp.minimum(c, steps - 1)
    prev = lambda c: jnp.maximum(c - 1, 0)
    const = lambda *shape: pl.BlockSpec(shape, lambda bi, c: (0,) * len(shape))
    in_specs = [pl.BlockSpec((None, rows, DN_CONV_CH), lambda bi, c: (bi, cur(c), 0)),
                pl.BlockSpec((None, rows, LANES), lambda bi, c: (bi, cur(c), 0)),
                pl.BlockSpec((rows, LANES), lambda bi, c: (cur(c), 0)),
                const(1, LANES), const(1, LANES), const(DN_HEADS, DN_DK, DN_DV)]
    out_specs = [pl.BlockSpec((None, rows, DN_V), lambda bi, c: (bi, prev(c), 0)),
                 pl.BlockSpec((None, DN_HEADS, DN_DK, DN_DV), lambda bi, c: (bi, 0, 0, 0))]
    out_shape = [jax.ShapeDtypeStruct((b, s, DN_V), BF16),
                 jax.ShapeDtypeStruct((b, DN_HEADS, DN_DK, DN_DV), F32)]
    scratch = [pltpu.VMEM((rows, DN_V), F32),
               pltpu.VMEM((rows, DN_QK), BF16),
               pltpu.VMEM((rows, DN_QK), BF16),
               pltpu.VMEM((rows, DN_QK), BF16),
               pltpu.VMEM((rows, DN_HEADS * CHUNK), BF16),
               pltpu.VMEM((cpb * DN_HEADS, DN_DV), F32),
               pltpu.VMEM((DN_HEADS, DN_DK, DN_DV), F32)]
    return pl.pallas_call(
        functools.partial(_deltanet_kernel, cpb=cpb),
        grid=(b, steps + 1), in_specs=in_specs, out_specs=out_specs, out_shape=out_shape,
        scratch_shapes=scratch,
        compiler_params=_cparams(("parallel", "arbitrary")), name="deltanet")(
            proj, small, valid_tab, pad_heads(a_log), pad_heads(dt_bias), s0)


def _layer_retention(h, pos, valid, chunk, s0, p):
    b, s, d = h.shape
    h2 = h.reshape(b * s, d)
    cos, sin = _rope_tables(pos)
    valid_tab = jnp.broadcast_to(valid[:, None], (s, LANES))
    proj2 = rms_inproj(h2, p["mix_norm_w"], p["w_in"], rope=(cos, sin, valid_tab))
    o, s_fin = retention(proj2.reshape(b, s, -1), s0, chunk)
    gate_blk = (2 * RET_QK + RET_V) // RET_V
    h2 = out_ffn(h2, o.reshape(b * s, RET_V), proj2, gate_blk, p["gn_w"], p["w_out"], p["ffn_norm_w"],
                 p["w_gate"], p["w_up"], p["w_down"])
    return h2.reshape(b, s, d), s_fin


def _layer_deltanet(h, valid, s0, halo0, p, final_w=None, run_tail=True):
    b, s, d = h.shape
    h2 = h.reshape(b * s, d)
    valid_tab = jnp.broadcast_to(valid[:, None], (s, LANES))
    proj2, small, tail = dn_inproj(h2, p["mix_norm_w"], p["w_in"], p["w_in_small"], p["conv_w"], halo0,
                                   valid_tab)
    o, s_fin = deltanet_mixer(proj2.reshape(b, s, -1), small.reshape(b, s, -1), valid, p["a_log"],
                              p["dt_bias"], s0)
    out = None
    if run_tail:
        gate_blk = DN_CONV_CH // DN_V
        out = out_ffn(h2, o.reshape(b * s, DN_V), proj2, gate_blk, p["norm_w"], p["w_out"],
                      p["ffn_norm_w"], p["w_gate"], p["w_up"], p["w_down"], final_w).reshape(b, s, d)
    return out, s_fin, tail


def kernel(x, meta_tokens, mix_norm_w, ffn_norm_w, ret_w_in, ret_gn_w, ret_w_out, dn_w_in, dn_conv_w,
           dn_a_log, dn_dt_bias, dn_norm_w, dn_w_out, ffn_w_gate, ffn_w_up, ffn_w_down, final_norm_w):
    b, s, d = x.shape
    bf = lambda t: t.astype(BF16)
    dn_main = DN_CONV_CH + DN_V
    small_w = jnp.zeros((d, LANES), F32).at[:, :2 * DN_HEADS].set(dn_w_in[0][:, dn_main:])
    p_ret = dict(mix_norm_w=mix_norm_w[0], w_in=bf(ret_w_in[0]), gn_w=ret_gn_w[0], w_out=bf(ret_w_out[0]),
                 ffn_norm_w=ffn_norm_w[0], w_gate=bf(ffn_w_gate[0]), w_up=bf(ffn_w_up[0]),
                 w_down=bf(ffn_w_down[0]))
    p_dn = dict(mix_norm_w=mix_norm_w[1], w_in=bf(dn_w_in[0][:, :dn_main]), w_in_small=bf(small_w),
                conv_w=dn_conv_w[0], a_log=dn_a_log[0], dt_bias=dn_dt_bias[0], norm_w=dn_norm_w[0],
                w_out=bf(dn_w_out[0]), ffn_norm_w=ffn_norm_w[1], w_gate=bf(ffn_w_gate[1]),
                w_up=bf(ffn_w_up[1]), w_down=bf(ffn_w_down[1]))

    h_meta = jnp.concatenate([jnp.zeros((PAD, d), x.dtype), meta_tokens.astype(x.dtype)], axis=0)[None]
    pos_meta = (jnp.arange(CHUNK) - PAD).astype(F32)
    valid_meta = (pos_meta >= 0).astype(F32)
    ret_s0 = jnp.zeros((RET_HEADS, RET_DK, RET_DV), F32)
    dn_s0 = jnp.zeros((DN_HEADS, DN_DK, DN_DV), F32)
    halo_zero = jnp.zeros((SUBLANES, DN_CONV_CH), F32)
    h_meta, ret_state = _layer_retention(h_meta, pos_meta, valid_meta, CHUNK, ret_s0, p_ret)
    _, dn_state, halo_meta = _layer_deltanet(h_meta, valid_meta, dn_s0, halo_zero, p_dn, run_tail=False)

    pos = (jnp.arange(s) + N_META).astype(F32)
    valid = jnp.ones((s,), F32)
    h, _ = _layer_retention(x, pos, valid, RET_CHUNK, ret_state[0], p_ret)
    out, _, _ = _layer_deltanet(h, valid, dn_state[0], halo_meta, p_dn, final_w=final_norm_w)
    return out
```

```python
import functools

import jax
import jax.numpy as jnp
from jax import lax
from jax.experimental import pallas as pl
from jax.experimental.pallas import tpu as pltpu

F32 = jnp.float32
BF16 = jnp.bfloat16

N_META = 16
CHUNK = 64
PAD = CHUNK - N_META
RMS_EPS = 1e-6
RET_HEADS = 4
RET_DK = 256
RET_DV = 512
RET_QK = RET_HEADS * RET_DK
RET_V = RET_HEADS * RET_DV
ROPE_BASE = 10000.0
DN_HEADS = 8
DN_DK = 128
DN_DV = 256
DN_QK = DN_HEADS * DN_DK
DN_V = DN_HEADS * DN_DV
DN_CONV_CH = 2 * DN_QK + DN_V
CONV_K = 4

LANES = 128
SUBLANES = 8
VMEM_LIMIT_BYTES = 56 * 1024 * 1024
TOKEN_BLOCK = 512
PROJ_N_CHUNK = 1024
FFN_ROW_SPLITS = 2
RET_CHUNK = 256
DN_CHUNKS_PER_STEP = 4


def _cparams(sem):
    return pltpu.CompilerParams(dimension_semantics=sem, vmem_limit_bytes=VMEM_LIMIT_BYTES)


def _resident(shape):
    nd = len(shape)
    return pl.BlockSpec(shape, lambda *_: (0,) * nd, pipeline_mode=pl.Buffered(1))


def _rms(x, w):
    ms = jnp.mean(x * x, axis=-1, keepdims=True)
    return x * lax.rsqrt(ms + RMS_EPS) * w


def _silu(x):
    h = 0.5 * x
    return h + h * jnp.tanh(h)


def _dot(a, b):
    return jnp.dot(a, b, preferred_element_type=F32)


def _dot_nt(a, b):
    return lax.dot_general(a, b, (((1,), (1,)), ((), ())), preferred_element_type=F32)


def _dot_tn(a, b):
    return lax.dot_general(a, b, (((0,), (0,)), ((), ())), preferred_element_type=F32)


def _rms_inproj_kernel(*refs, has_rope):
    if has_rope:
        h_ref, nw_ref, w_ref, cos_ref, sin_ref, valid_ref, o_ref = refs
    else:
        h_ref, nw_ref, w_ref, o_ref = refs
    xn = _rms(h_ref[...], nw_ref[...]).astype(BF16)
    n_total = o_ref.shape[-1]
    for n0 in range(0, n_total, PROJ_N_CHUNK):
        n1 = min(n0 + PROJ_N_CHUNK, n_total)
        res = _dot(xn, w_ref[:, n0:n1])
        if has_rope and n0 < 2 * RET_QK:
            cos, sin = cos_ref[...], sin_ref[...]
            half = RET_DK // 2
            parts = []
            for c0 in range(0, n1 - n0, RET_DK):
                t1, t2 = res[:, c0:c0 + half], res[:, c0 + half:c0 + RET_DK]
                r1, r2 = t1 * cos - t2 * sin, t1 * sin + t2 * cos
                if n0 + c0 >= RET_QK:
                    valid = valid_ref[...]
                    r1 = r1 * (RET_DK ** -0.5) * valid
                    r2 = r2 * (RET_DK ** -0.5) * valid
                parts += [r1, r2]
            res = jnp.concatenate(parts, axis=1)
        o_ref[:, n0:n1] = res.astype(o_ref.dtype)


def rms_inproj(h2d, norm_w, w, rope=None):
    m, d = h2d.shape
    n = w.shape[1]
    bm = min(TOKEN_BLOCK, m)
    has_rope = rope is not None
    in_specs = [pl.BlockSpec((bm, d), lambda i: (i, 0)), _resident((1, d)), _resident((d, n))]
    args = [h2d, norm_w.reshape(1, d), w]
    if has_rope:
        assert PROJ_N_CHUNK % RET_DK == 0 and RET_QK % PROJ_N_CHUNK == 0
        blocks_per_seq = rope[0].shape[0] // bm
        in_specs += [pl.BlockSpec((bm, LANES), lambda i: (i % blocks_per_seq, 0))] * 3
        args += list(rope)
    return pl.pallas_call(
        functools.partial(_rms_inproj_kernel, has_rope=has_rope),
        grid=(m // bm,), in_specs=in_specs, out_specs=pl.BlockSpec((bm, n), lambda i: (i, 0)),
        out_shape=jax.ShapeDtypeStruct((m, n), BF16),
        compiler_params=_cparams(("parallel",)), name="rms_inproj")(*args)


def _dn_inproj_kernel(h_ref, nw_ref, w_ref, ws_ref, convw_ref, halo0_ref, valid_ref,
                      o_ref, os_ref, tail_ref, carry, *, blocks_per_seq):
    i = pl.program_id(0)

    @pl.when(i % blocks_per_seq == 0)
    def _():
        carry[...] = halo0_ref[...]

    xn = _rms(h_ref[...], nw_ref[...]).astype(BF16)
    bm = h_ref.shape[0]
    valid = valid_ref[...]
    n_total = o_ref.shape[-1]
    chunks = [(n0, min(n0 + PROJ_N_CHUNK, n_total)) for n0 in range(0, n_total, PROJ_N_CHUNK)]
    res_next = _dot(xn, w_ref[:, chunks[0][0]:chunks[0][1]])
    outs = []
    for ci, (n0, n1) in enumerate(chunks):
        res = res_next
        if ci + 1 < len(chunks):
            res_next = _dot(xn, w_ref[:, chunks[ci + 1][0]:chunks[ci + 1][1]])
        if n0 < DN_CONV_CH:
            x = res * jnp.concatenate([valid] * ((n1 - n0) // LANES), axis=1)
            ext = jnp.concatenate([carry[:, n0:n1], x], axis=0)
            acc = x * convw_ref[CONV_K - 1:CONV_K, n0:n1]
            for j in range(CONV_K - 1):
                delayed = pltpu.roll(ext, shift=CONV_K - 1 - j, axis=0)[SUBLANES:]
                acc = acc + delayed * convw_ref[j:j + 1, n0:n1]
            carry[:, n0:n1] = x[bm - SUBLANES:, :]
            res = _silu(acc)
            if n0 < 2 * DN_QK:
                parts = []
                for c0 in range(0, n1 - n0, DN_DK):
                    y = res[:, c0:c0 + DN_DK]
                    y = y * lax.rsqrt(jnp.sum(y * y, axis=-1, keepdims=True) + RMS_EPS)
                    parts.append(y * (DN_DK ** -0.5) if n0 < DN_QK else y)
                res = jnp.concatenate(parts, axis=1)
        outs.append(res.astype(o_ref.dtype))
    o_ref[...] = jnp.concatenate(outs, axis=1)
    os_ref[...] = _dot(xn, ws_ref[...])

    @pl.when(i == pl.num_programs(0) - 1)
    def _():
        tail_ref[...] = carry[...]


def dn_inproj(h2d, norm_w, w, w_small, conv_w, halo0, valid_tab):
    m, d = h2d.shape
    n = w.shape[1]
    ns = w_small.shape[1]
    bm = min(TOKEN_BLOCK, m)
    assert DN_QK % PROJ_N_CHUNK == 0 and DN_CONV_CH % PROJ_N_CHUNK == 0 and PROJ_N_CHUNK % DN_DK == 0
    blocks_per_seq = valid_tab.shape[0] // bm
    row = lambda width: pl.BlockSpec((bm, width), lambda i: (i, 0))
    in_specs = [row(d), _resident((1, d)), _resident((d, n)), _resident((d, ns)),
                _resident((CONV_K, DN_CONV_CH)), _resident((SUBLANES, DN_CONV_CH)),
                pl.BlockSpec((bm, LANES), lambda i: (i % blocks_per_seq, 0))]
    out_specs = [row(n), row(ns), pl.BlockSpec((SUBLANES, DN_CONV_CH), lambda i: (0, 0))]
    out_shape = [jax.ShapeDtypeStruct((m, n), BF16), jax.ShapeDtypeStruct((m, ns), F32),
                 jax.ShapeDtypeStruct((SUBLANES, DN_CONV_CH), F32)]
    return pl.pallas_call(
        functools.partial(_dn_inproj_kernel, blocks_per_seq=blocks_per_seq),
        grid=(m // bm,), in_specs=in_specs, out_specs=out_specs, out_shape=out_shape,
        scratch_shapes=[pltpu.VMEM((SUBLANES, DN_CONV_CH), F32)],
        compiler_params=_cparams(("arbitrary",)), name="dn_inproj")(
            h2d, norm_w.reshape(1, d), w, w_small, conv_w, halo0, valid_tab)


def _out_ffn_kernel(*refs, final, head_dv):
    if final:
        h_ref, o_ref, g_ref, gnw_ref, wo_ref, nw_ref, wg_ref, wu_ref, wd_ref, fw_ref, out_ref = refs
    else:
        h_ref, o_ref, g_ref, gnw_ref, wo_ref, nw_ref, wg_ref, wu_ref, wd_ref, out_ref = refs
    gnw = gnw_ref[...]
    bm = h_ref.shape[0]
    n_sub = FFN_ROW_SPLITS if bm % (FFN_ROW_SPLITS * SUBLANES * 2) == 0 else 1
    outs = []
    for r0 in range(0, bm, bm // n_sub):
        rs = slice(r0, r0 + bm // n_sub)
        h1 = h_ref[rs, :]
        for c0 in range(0, o_ref.shape[1], head_dv):
            x = o_ref[rs, c0:c0 + head_dv].astype(F32)
            xn = x * lax.rsqrt(jnp.mean(x * x, axis=-1, keepdims=True) + RMS_EPS) * gnw
            og = (xn * _silu(g_ref[rs, c0:c0 + head_dv].astype(F32))).astype(BF16)
            h1 = h1 + _dot(og, wo_ref[c0:c0 + head_dv, :])
        xn = _rms(h1, nw_ref[...]).astype(BF16)
        act = (_silu(_dot(xn, wg_ref[...])) * _dot(xn, wu_ref[...])).astype(BF16)
        h2 = h1 + _dot(act, wd_ref[...])
        if final:
            h2 = _rms(h2, fw_ref[...])
        outs.append(h2)
    out_ref[...] = jnp.concatenate(outs, axis=0)


def out_ffn(h2d, o2d, proj2d, gate_blk, gn_w, w_out, norm_w, w_gate, w_up, w_down, final_w=None):
    m, d = h2d.shape
    dv = o2d.shape[1]
    head_dv = gn_w.shape[0]
    f = w_gate.shape[1]
    bm = min(TOKEN_BLOCK, m)
    final = final_w is not None
    row = lambda width, blk=0: pl.BlockSpec((bm, width), lambda i: (i, blk))
    in_specs = [row(d), row(dv), row(dv, gate_blk), _resident((1, head_dv)), _resident((dv, d)),
                _resident((1, d)), _resident((d, f)), _resident((d, f)), _resident((f, d))]
    args = [h2d, o2d, proj2d, gn_w.reshape(1, head_dv), w_out, norm_w.reshape(1, d), w_gate, w_up, w_down]
    if final:
        in_specs.append(_resident((1, d)))
        args.append(final_w.reshape(1, d))
    return pl.pallas_call(
        functools.partial(_out_ffn_kernel, final=final, head_dv=head_dv),
        grid=(m // bm,), in_specs=in_specs, out_specs=row(d),
        out_shape=jax.ShapeDtypeStruct((m, d), F32),
        compiler_params=_cparams(("parallel",)), name="out_ffn")(*args)


def _retention_kernel(qk_ref, v_ref, dmask_ref, xi_ref, zeta_ref, gc_ref, s0_ref, o_ref, sfin_ref,
                      state):
    c = pl.program_id(1)

    @pl.when(c == 0)
    def _():
        state[...] = s0_ref[...]

    heads = range(RET_HEADS)
    wide = lambda tab: jnp.concatenate([tab] * (RET_DK // LANES), axis=1)
    qb = [qk_ref[:, h * RET_DK:(h + 1) * RET_DK] for h in heads]
    kb = [qk_ref[:, RET_QK + h * RET_DK:RET_QK + (h + 1) * RET_DK] for h in heads]
    qx = [(qb[h].astype(F32) * wide(xi_ref[h])).astype(BF16) for h in heads]
    kz = [(kb[h].astype(F32) * wide(zeta_ref[h])).astype(BF16) for h in heads]
    vb = [v_ref[:, h * RET_DV:(h + 1) * RET_DV] for h in heads]
    s_prev = [state[h] for h in heads]
    scores = [(_dot_nt(qb[h], kb[h]) * dmask_ref[h]).astype(BF16) for h in heads]
    o = [_dot(scores[h], vb[h]) + _dot(qx[h], s_prev[h].astype(BF16)) for h in heads]
    s_new = [gc_ref[h, 0:1, :] * s_prev[h] + _dot_tn(kz[h], vb[h]) for h in heads]
    state[...] = jnp.stack(s_new, axis=0)
    o_ref[...] = jnp.concatenate(o, axis=1).astype(o_ref.dtype)

    @pl.when(c == pl.num_programs(1) - 1)
    def _():
        sfin_ref[...] = state[...]


def _retention_tables(chunk):
    log_gamma = jnp.log1p(-jnp.exp2(-5.0 - jnp.arange(RET_HEADS, dtype=F32)))
    idx = jnp.arange(chunk, dtype=F32)
    rel = idx[:, None] - idx[None, :]
    dmask = jnp.where((rel >= 0)[None],
                      jnp.exp(log_gamma[:, None, None] * jnp.maximum(rel, 0.0)), 0.0)
    xi = jnp.exp(log_gamma[:, None] * (idx[None, :] + 1.0))
    zeta = jnp.exp(log_gamma[:, None] * (chunk - 1.0 - idx[None, :]))
    gamma_c = jnp.exp(log_gamma * chunk)
    xi = jnp.broadcast_to(xi[:, :, None], (RET_HEADS, chunk, LANES))
    zeta = jnp.broadcast_to(zeta[:, :, None], (RET_HEADS, chunk, LANES))
    gc = jnp.broadcast_to(gamma_c[:, None, None], (RET_HEADS, SUBLANES, RET_DV))
    return dmask, xi, zeta, gc


def _rope_tables(pos):
    half = RET_DK // 2
    inv_freq = ROPE_BASE ** (-jnp.arange(half, dtype=F32) / half)
    ang = pos[:, None] * inv_freq[None, :]
    return jnp.cos(ang), jnp.sin(ang)


def retention(proj, s0, chunk):
    b, s, _ = proj.shape
    nc = s // chunk
    dmask, xi, zeta, gc = _retention_tables(chunk)
    assert 2 * RET_QK == RET_V
    tok = lambda blk: pl.BlockSpec((None, chunk, RET_V), lambda bi, c: (bi, c, blk))
    whole = lambda *shape: pl.BlockSpec(shape, lambda bi, c: (0,) * len(shape))
    in_specs = [tok(0), tok(1),
                whole(RET_HEADS, chunk, chunk), whole(RET_HEADS, chunk, LANES),
                whole(RET_HEADS, chunk, LANES), whole(RET_HEADS, SUBLANES, RET_DV),
                whole(RET_HEADS, RET_DK, RET_DV)]
    out_specs = [tok(0),
                 pl.BlockSpec((None, RET_HEADS, RET_DK, RET_DV), lambda bi, c: (bi, 0, 0, 0))]
    out_shape = [jax.ShapeDtypeStruct((b, s, RET_V), BF16),
                 jax.ShapeDtypeStruct((b, RET_HEADS, RET_DK, RET_DV), F32)]
    return pl.pallas_call(
        _retention_kernel, grid=(b, nc), in_specs=in_specs, out_specs=out_specs,
        out_shape=out_shape, scratch_shapes=[pltpu.VMEM((RET_HEADS, RET_DK, RET_DV), F32)],
        compiler_params=_cparams(("parallel", "arbitrary")), name="retention")(
            proj, proj, dmask, xi, zeta, gc, s0)


def _split(x):
    hi = x.astype(BF16)
    lo = (x - hi.astype(F32)).astype(BF16)
    return hi, lo


def _pair_block_diag(y, low):
    zero = jnp.zeros_like(y)
    return jnp.concatenate([jnp.where(low, y, zero), jnp.where(low, zero, y)], axis=0)


def _mm_pair(xs, ys, low):
    (xh, xl), (yh, yl) = xs, ys
    lhs = jnp.concatenate([xh, xh, xl], axis=1)
    rhs = jnp.concatenate([_pair_block_diag(yh, low), _pair_block_diag(yl, low),
                           _pair_block_diag(yh, low)], axis=0)
    return _dot(lhs, rhs)


def _unit_lower_inverses(a_list, row, col, low):
    eye = (row == col).astype(F32)
    blk = lambda n: (row // n) == (col // n)
    mm = functools.partial(_mm_pair, low=low)
    sq = lambda xs: [mm(x, x) for x in xs]
    n1 = [jnp.where(blk(16), -a, 0.0) for a in a_list]
    n2 = sq([_split(x) for x in n1])
    n4 = sq([_split(x) for x in n2])
    mm1 = lambda xh, yh: _dot(xh, _pair_block_diag(yh, low))
    n8 = [mm1(x.astype(BF16), x.astype(BF16)) for x in n4]
    t = [mm(_split(eye + a), _split(eye + b)) for a, b in zip(n1, n2)]
    t = [mm(_split(x), _split(eye + y)) for x, y in zip(t, n4)]
    t = [mm1(x.astype(BF16), (eye + y).astype(BF16)) for x, y in zip(t, n8)]
    for n in (32, 64):
        off_mask = blk(n) & ~blk(n // 2)
        th = [x.astype(BF16) for x in t]
        to = [mm1(x, jnp.where(off_mask, a, 0.0).astype(BF16)) for x, a in zip(th, a_list)]
        t = [x - mm1(y.astype(BF16), xh) for x, y, xh in zip(t, to, th)]
    return [_split(x) for x in t]


def _dn_recurrence(u_s, w_s, qg_s, kd_s, attn_s, eg_s, state, cpb):
    heads = range(DN_HEADS)
    dk = [slice(h * DN_DK, (h + 1) * DN_DK) for h in heads]
    dv = [slice(h * DN_DV, (h + 1) * DN_DV) for h in heads]
    s_cur = [state[h] for h in heads]
    out_rows = []
    for j in range(cpb):
        rs = slice(j * CHUNK, (j + 1) * CHUNK)
        wq = [jnp.concatenate([w_s[rs, dk[h]], qg_s[rs, dk[h]]], axis=0) for h in heads]
        ws = [_dot(wq[h], s_cur[h].astype(BF16)) for h in heads]
        vb = [(u_s[rs, dv[h]] - ws[h][:CHUNK]).astype(BF16) for h in heads]
        o = [ws[h][CHUNK:] + _dot(attn_s[rs, h * CHUNK:(h + 1) * CHUNK], vb[h]) for h in heads]
        s_cur = [s_cur[h] * eg_s[j * DN_HEADS + h:j * DN_HEADS + h + 1, :] + _dot_tn(kd_s[rs, dk[h]], vb[h])
                 for h in heads]
        out_rows.append(jnp.concatenate(o, axis=1))
    state[...] = jnp.stack(s_cur, axis=0)
    return jnp.concatenate(out_rows, axis=0)


def _deltanet_kernel(qkv_ref, small_ref, valid_ref, alog_ref, dtb_ref, s0_ref, o_ref, sfin_ref,
                     u_ref, w_ref, qg_ref, kd_ref, attn_ref, eg_ref, state, *, cpb):
    c = pl.program_id(1)

    @pl.when(c == 0)
    def _():
        for ref in (u_ref, w_ref, qg_ref, kd_ref, attn_ref, eg_ref):
            ref[...] = jnp.zeros(ref.shape, ref.dtype)

    @pl.when(c <= 1)
    def _():
        state[...] = s0_ref[...]

    o_ref[...] = _dn_recurrence(u_ref, w_ref, qg_ref, kd_ref, attn_ref, eg_ref, state,
                                cpb).astype(o_ref.dtype)

    row = lax.broadcasted_iota(jnp.int32, (CHUNK, CHUNK), 0)
    col = lax.broadcasted_iota(jnp.int32, (CHUNK, CHUNK), 1)
    tri_l = (row >= col).astype(BF16)
    tri_u = (row <= col).astype(BF16)

    def split3(x):
        hi, lo = _split(x)
        lo2 = (x - hi.astype(F32) - lo.astype(F32)).astype(BF16)
        return hi, lo, lo2

    beta_all, gam_all, gam_rows, egam_all, ekd_all = [], [], [], [], []
    for j in range(cpb):
        rs = slice(j * CHUNK, (j + 1) * CHUNK)
        valid = valid_ref[rs, :]
        small = small_ref[rs, :]
        beta_all.append(jax.nn.sigmoid(small) * valid)
        g_all = -jnp.exp(alog_ref[...]) * jax.nn.softplus(small + dtb_ref[...]) * valid
        gam = _dot(jnp.concatenate([tri_l] * 3, axis=1), jnp.concatenate(split3(g_all), axis=0))
        g_rows = g_all.T[DN_HEADS:2 * DN_HEADS, :]
        gam_rows.append(_dot(jnp.concatenate(split3(g_rows), axis=1), jnp.concatenate([tri_u] * 3, axis=0)))
        gam_all.append(gam)
        egam_all.append(jnp.exp(gam))
        ekd_all.append(jnp.exp(gam[CHUNK - 1:CHUNK, :] - gam))

    pairs = range(DN_HEADS // 2)
    units = [(j, p) for j in range(cpb) for p in pairs]
    lane_c = lax.broadcasted_iota(jnp.int32, (CHUNK, 2 * CHUNK), 1)
    prow = lax.broadcasted_iota(jnp.int32, (CHUNK, 2 * CHUNK), 0)
    pcol = lane_c & (CHUNK - 1)
    low_c = lane_c < CHUNK
    low_k = lax.broadcasted_iota(jnp.int32, (CHUNK, 2 * DN_DK), 1) < DN_DK
    p_incl = prow >= pcol
    p_strict = prow > pcol

    def head_cols(x_all, off, p, low):
        a = x_all[:, off + 2 * p:off + 2 * p + 1]
        b = x_all[:, off + 2 * p + 1:off + 2 * p + 2]
        return jnp.where(low, a, b)

    slab = lambda j, lo, width: qkv_ref[j * CHUNK:(j + 1) * CHUNK, lo:lo + width]
    qb = [slab(j, p * 2 * DN_DK, 2 * DN_DK) for j, p in units]
    kb = [slab(j, DN_QK + p * 2 * DN_DK, 2 * DN_DK) for j, p in units]
    q = [x.astype(F32) for x in qb]
    k = [x.astype(F32) for x in kb]
    v = [slab(j, 2 * DN_QK + p * 2 * DN_DV, 2 * DN_DV).astype(F32) for j, p in units]
    beta_c = [head_cols(beta_all[j], 0, p, low_c) for j, p in units]
    gcol = [head_cols(gam_all[j], DN_HEADS, p, low_c) for j, p in units]
    grow = [jnp.concatenate([gam_rows[j][2 * p:2 * p + 1, :], gam_rows[j][2 * p + 1:2 * p + 2, :]], axis=1)
            for j, p in units]
    decay = [jnp.exp(jnp.where(p_incl, gc - gr, -jnp.inf)) for gc, gr in zip(gcol, grow)]
    zk = jnp.zeros((CHUNK, 2 * DN_DK), BF16)
    kq = [_dot_nt(jnp.concatenate([kbp, qbp], axis=0),
                  jnp.concatenate([jnp.where(low_k, kbp, zk), jnp.where(low_k, zk, kbp)], axis=0))
          for kbp, qbp in zip(kb, qb)]
    a_mat = [jnp.where(p_strict, b * x[:CHUNK] * d, 0.0) for b, x, d in zip(beta_c, kq, decay)]
    t = _unit_lower_inverses(a_mat, prow, pcol, low_c)

    zr = jnp.zeros((CHUNK, DN_DV + DN_DK), F32)
    uw = []
    for n, (j, p) in enumerate(units):
        r = []
        for i in range(2):
            h = 2 * p + i
            b = beta_all[j][:, h:h + 1]
            e = egam_all[j][:, DN_HEADS + h:DN_HEADS + h + 1]
            r.append(jnp.concatenate([v[n][:, i * DN_DV:(i + 1) * DN_DV] * b,
                                      k[n][:, i * DN_DK:(i + 1) * DN_DK] * (b * e)], axis=1))
        rhs = jnp.concatenate([jnp.concatenate([r[0], zr], axis=1),
                               jnp.concatenate([zr, r[1]], axis=1)], axis=0).astype(BF16)
        th, tl = t[n]
        uw.append(_dot(jnp.concatenate([th, tl], axis=1), jnp.concatenate([rhs, rhs], axis=0)))

    dvk = DN_DV + DN_DK
    n_pairs = len(pairs)
    rows = lambda per_chunk: jnp.concatenate([per_chunk(j) for j in range(cpb)], axis=0)
    lanes = lambda parts: jnp.concatenate(parts, axis=1)
    u_ref[...] = rows(lambda j: lanes([uw[j * n_pairs + p][:, i * dvk:i * dvk + DN_DV]
                                       for p in pairs for i in range(2)]))
    w_ref[...] = rows(lambda j: lanes([uw[j * n_pairs + p][:, i * dvk + DN_DV:(i + 1) * dvk]
                                       for p in pairs for i in range(2)])).astype(BF16)
    qg_ref[...] = rows(lambda j: lanes([q[j * n_pairs + p] * head_cols(egam_all[j], DN_HEADS, p, low_k)
                                        for p in pairs])).astype(BF16)
    kd_ref[...] = rows(lambda j: lanes([k[j * n_pairs + p] * head_cols(ekd_all[j], DN_HEADS, p, low_k)
                                        for p in pairs])).astype(BF16)
    attn_ref[...] = rows(lambda j: lanes([kq[j * n_pairs + p][CHUNK:] * decay[j * n_pairs + p]
                                          for p in pairs])).astype(BF16)
    eg_ref[...] = rows(lambda j: jnp.broadcast_to(jnp.exp(gam_rows[j][:, CHUNK - 1:CHUNK]),
                                                  (DN_HEADS, DN_DV)))

    @pl.when(c == pl.num_programs(1) - 1)
    def _():
        sfin_ref[...] = state[...]


def deltanet_mixer(proj, small, valid, a_log, dt_bias, s0):
    b, s, _ = proj.shape
    nc = s // CHUNK
    cpb = DN_CHUNKS_PER_STEP if nc % DN_CHUNKS_PER_STEP == 0 else 1
    rows = cpb * CHUNK
    steps = nc // cpb
    valid_tab = jnp.broadcast_to(valid[:, None], (s, LANES))
    pad_heads = lambda v: jnp.zeros((1, LANES), F32).at[0, DN_HEADS:2 * DN_HEADS].set(v)
    cur = lambda c: jnp.minimum(c, steps - 1)
    prev = lambda c: jnp.maximum(c - 1, 0)
    const = lambda *shape: pl.BlockSpec(shape, lambda bi, c: (0,) * len(shape))
    in_specs = [pl.BlockSpec((None, rows, DN_CONV_CH), lambda bi, c: (bi, cur(c), 0)),
                pl.BlockSpec((None, rows, LANES), lambda bi, c: (bi, cur(c), 0)),
                pl.BlockSpec((rows, LANES), lambda bi, c: (cur(c), 0)),
                const(1, LANES), const(1, LANES), const(DN_HEADS, DN_DK, DN_DV)]
    out_specs = [pl.BlockSpec((None, rows, DN_V), lambda bi, c: (bi, prev(c), 0)),
                 pl.BlockSpec((None, DN_HEADS, DN_DK, DN_DV), lambda bi, c: (bi, 0, 0, 0))]
    out_shape = [jax.ShapeDtypeStruct((b, s, DN_V), BF16),
                 jax.ShapeDtypeStruct((b, DN_HEADS, DN_DK, DN_DV), F32)]
    scratch = [pltpu.VMEM((rows, DN_V), F32),
               pltpu.VMEM((rows, DN_QK), BF16),
               pltpu.VMEM((rows, DN_QK), BF16),
               pltpu.VMEM((rows, DN_QK), BF16),
               pltpu.VMEM((rows, DN_HEADS * CHUNK), BF16),
               pltpu.VMEM((cpb * DN_HEADS, DN_DV), F32),
               pltpu.VMEM((DN_HEADS, DN_DK, DN_DV), F32)]
    return pl.pallas_call(
        functools.partial(_deltanet_kernel, cpb=cpb),
        grid=(b, steps + 1), in_specs=in_specs, out_specs=out_specs, out_shape=out_shape,
        scratch_shapes=scratch,
        compiler_params=_cparams(("parallel", "arbitrary")), name="deltanet")(
            proj, small, valid_tab, pad_heads(a_log), pad_heads(dt_bias), s0)


def _layer_retention(h, pos, valid, chunk, s0, p):
    b, s, d = h.shape
    h2 = h.reshape(b * s, d)
    cos, sin = _rope_tables(pos)
    valid_tab = jnp.broadcast_to(valid[:, None], (s, LANES))
    proj2 = rms_inproj(h2, p["mix_norm_w"], p["w_in"], rope=(cos, sin, valid_tab))
    o, s_fin = retention(proj2.reshape(b, s, -1), s0, chunk)
    gate_blk = (2 * RET_QK + RET_V) // RET_V
    h2 = out_ffn(h2, o.reshape(b * s, RET_V), proj2, gate_blk, p["gn_w"], p["w_out"], p["ffn_norm_w"],
                 p["w_gate"], p["w_up"], p["w_down"])
    return h2.reshape(b, s, d), s_fin


def _layer_deltanet(h, valid, s0, halo0, p, final_w=None, run_tail=True):
    b, s, d = h.shape
    h2 = h.reshape(b * s, d)
    valid_tab = jnp.broadcast_to(valid[:, None], (s, LANES))
    proj2, small, tail = dn_inproj(h2, p["mix_norm_w"], p["w_in"], p["w_in_small"], p["conv_w"], halo0,
                                   valid_tab)
    o, s_fin = deltanet_mixer(proj2.reshape(b, s, -1), small.reshape(b, s, -1), valid, p["a_log"],
                              p["dt_bias"], s0)
    out = None
    if run_tail:
        gate_blk = DN_CONV_CH // DN_V
        out = out_ffn(h2, o.reshape(b * s, DN_V), proj2, gate_blk, p["norm_w"], p["w_out"],
                      p["ffn_norm_w"], p["w_gate"], p["w_up"], p["w_down"], final_w).reshape(b, s, d)
    return out, s_fin, tail


def kernel(x, meta_tokens, mix_norm_w, ffn_norm_w, ret_w_in, ret_gn_w, ret_w_out, dn_w_in, dn_conv_w,
           dn_a_log, dn_dt_bias, dn_norm_w, dn_w_out, ffn_w_gate, ffn_w_up, ffn_w_down, final_norm_w):
    b, s, d = x.shape
    bf = lambda t: t.astype(BF16)
    dn_main = DN_CONV_CH + DN_V
    small_w = jnp.zeros((d, LANES), F32).at[:, :2 * DN_HEADS].set(dn_w_in[0][:, dn_main:])
    p_ret = dict(mix_norm_w=mix_norm_w[0], w_in=bf(ret_w_in[0]), gn_w=ret_gn_w[0], w_out=bf(ret_w_out[0]),
                 ffn_norm_w=ffn_norm_w[0], w_gate=bf(ffn_w_gate[0]), w_up=bf(ffn_w_up[0]),
                 w_down=bf(ffn_w_down[0]))
    p_dn = dict(mix_norm_w=mix_norm_w[1], w_in=bf(dn_w_in[0][:, :dn_main]), w_in_small=bf(small_w),
                conv_w=dn_conv_w[0], a_log=dn_a_log[0], dt_bias=dn_dt_bias[0], norm_w=dn_norm_w[0],
                w_out=bf(dn_w_out[0]), ffn_norm_w=ffn_norm_w[1], w_gate=bf(ffn_w_gate[1]),
                w_up=bf(ffn_w_up[1]), w_down=bf(ffn_w_down[1]))

    h_meta = jnp.concatenate([jnp.zeros((PAD, d), x.dtype), meta_tokens.astype(x.dtype)], axis=0)[None]
    pos_meta = (jnp.arange(CHUNK) - PAD).astype(F32)
    valid_meta = (pos_meta >= 0).astype(F32)
    ret_s0 = jnp.zeros((RET_HEADS, RET_DK, RET_DV), F32)
    dn_s0 = jnp.zeros((DN_HEADS, DN_DK, DN_DV), F32)
    halo_zero = jnp.zeros((SUBLANES, DN_CONV_CH), F32)
    h_meta, ret_state = _layer_retention(h_meta, pos_meta, valid_meta, CHUNK, ret_s0, p_ret)
    _, dn_state, halo_meta = _layer_deltanet(h_meta, valid_meta, dn_s0, halo_zero, p_dn, run_tail=False)

    pos = (jnp.arange(s) + N_META).astype(F32)
    valid = jnp.ones((s,), F32)
    h, _ = _layer_retention(x, pos, valid, RET_CHUNK, ret_state[0], p_ret)
    out, _, _ = _layer_deltanet(h, valid, dn_state[0], halo_meta, p_dn, final_w=final_norm_w)
    return out
```
